```python
import jax, jax.numpy as jnp
from jax import lax
import numpy as np


D_MODEL = 1024
BATCH = 32
SEQ = 2048
DEPTH = 1
DEC_BATCH = 16
DEC_SEQ = 4096
PAST_LEN = 128

HEAD_DIM = 64
ROPE_DIM = HEAD_DIM // 4
ROPE_THETA = 500000.0
EPS = 1e-6
A_Q_HEADS = 8
A_KV_HEADS = 2
A_GROUP = A_Q_HEADS // A_KV_HEADS
A_WINDOW = 128
B_PATTERNS = ((128, 1), (512, 4), (2048, 16))
B_HEADS_PER_GROUP = 4
B_HEADS = B_HEADS_PER_GROUP * len(B_PATTERNS)
A_Q_W = A_Q_HEADS * HEAD_DIM
A_KV_W = A_KV_HEADS * HEAD_DIM
B_W = B_HEADS * HEAD_DIM
B_OUT_W = B_HEADS_PER_GROUP * HEAD_DIM
IN_COLS = A_Q_W + 2 * A_KV_W + 3 * B_W + 2 * D_MODEL
SPLITS = tuple(int(s) for s in np.cumsum([A_Q_W, A_KV_W, A_KV_W, B_W, B_W, B_W, D_MODEL]))
PEER_HEADS = 8
PEER_KEYS = 128
PEER_EXPERTS = PEER_KEYS * PEER_KEYS
PEER_HALF = 128
PEER_TOPK = 16
PEER_CHUNK = 128

kernel_name = "hybrid_gated_swa_dilated_peer_encoder"


def rms_norm(x, g):
    xf = x.astype(jnp.float32)
    y = xf * lax.rsqrt(jnp.mean(xf * xf, axis=-1, keepdims=True) + EPS)
    return (y * g.astype(jnp.float32)).astype(x.dtype)


def rope_partial(t):
    S = t.shape[1]
    half = ROPE_DIM // 2
    inv = ROPE_THETA ** (-jnp.arange(0, ROPE_DIM, 2, dtype=jnp.float32) / ROPE_DIM)
    ang = jnp.arange(S, dtype=jnp.float32)[:, None] * inv[None, :]
    cos = jnp.cos(ang)[None, :, None, :]
    sin = jnp.sin(ang)[None, :, None, :]
    x1 = t[..., :half].astype(jnp.float32)
    x2 = t[..., half:ROPE_DIM].astype(jnp.float32)
    rot = jnp.concatenate([x1 * cos - x2 * sin, x2 * cos + x1 * sin], axis=-1)
    return jnp.concatenate([rot.astype(t.dtype), t[..., ROPE_DIM:]], axis=-1)


def qk_prep(t, g):
    return rope_partial(rms_norm(t, g))


def banded_attention(q, k, v, w, sink=None):
    B, L, KVH, G, Dh = q.shape
    nb = -(-L // w)
    Lp = nb * w
    qp = jnp.pad(q, ((0, 0), (0, Lp - L), (0, 0), (0, 0), (0, 0)))
    pad_k = ((0, 0), (w, Lp - L + w), (0, 0), (0, 0))
    kp = jnp.pad(k, pad_k)
    vp = jnp.pad(v, pad_k)
    kb = jnp.concatenate([kp[:, i * w:i * w + Lp].reshape(B, nb, w, KVH, Dh) for i in range(3)], axis=2)
    vb = jnp.concatenate([vp[:, i * w:i * w + Lp].reshape(B, nb, w, KVH, Dh) for i in range(3)], axis=2)
    qb = qp.reshape(B, nb, w, KVH, G, Dh)
    s = jnp.einsum('bnqkgd,bnpkd->bnkgqp', qb, kb, preferred_element_type=jnp.float32) * (Dh ** -0.5)
    blk = jnp.arange(nb)[:, None, None]
    qpos = blk * w + jnp.arange(w)[None, :, None]
    kpos = (blk - 1) * w + jnp.arange(3 * w)[None, None, :]
    mask = (jnp.abs(kpos - qpos) <= w) & (kpos >= 0) & (kpos < L)
    s = jnp.where(mask[None, :, None, None, :, :], s, -1e30)
    m = jnp.max(s, axis=-1)
    if sink is not None:
        sk = sink.astype(jnp.float32).reshape(KVH, G)[None, None, :, :, None]
        m = jnp.maximum(m, sk)
    p = jnp.exp(s - m[..., None])
    denom = jnp.sum(p, axis=-1)
    if sink is not None:
        denom = denom + jnp.exp(sk - m)
    o = jnp.einsum('bnkgqp,bnpkd->bnkgqd', p, vb.astype(jnp.float32)) / denom[..., None]
    lse = m + jnp.log(denom)
    o = o.transpose(0, 1, 4, 2, 3, 5).reshape(B, Lp, KVH, G, Dh)[:, :L]
    lse = lse.transpose(0, 1, 4, 2, 3).reshape(B, Lp, KVH, G)[:, :L]
    return o, lse


def dilated_attention(q, k, v, window, dilation):
    B, S, H, Dh = q.shape
    L = S // dilation
    hw = window // (2 * dilation)

    def to_sub(t):
        return t.reshape(B, L, dilation, H, Dh).transpose(0, 2, 1, 3, 4).reshape(B * dilation, L, H, Dh)

    o, lse = banded_attention(to_sub(q)[:, :, :, None, :], to_sub(k), to_sub(v), hw)
    o = o[:, :, :, 0].reshape(B, dilation, L, H, Dh).transpose(0, 2, 1, 3, 4).reshape(B, S, H, Dh)
    lse = lse[..., 0].reshape(B, dilation, L, H).transpose(0, 2, 1, 3).reshape(B, S, H)
    return o, lse


def peer(h, w_query, sub_keys_1, sub_keys_2, expert_u, expert_v):
    B, S, D = h.shape
    t = h.reshape(B * S, D)
    T = t.shape[0]
    q = (t @ w_query).reshape(T, PEER_HEADS, 2, PEER_HALF)
    s1 = jnp.einsum('thc,hkc->thk', q[:, :, 0], sub_keys_1, preferred_element_type=jnp.float32)
    s2 = jnp.einsum('thc,hkc->thk', q[:, :, 1], sub_keys_2, preferred_element_type=jnp.float32)
    v1, i1 = lax.top_k(s1, PEER_TOPK)
    v2, i2 = lax.top_k(s2, PEER_TOPK)
    cand = (v1[..., :, None] + v2[..., None, :]).reshape(T, PEER_HEADS, PEER_TOPK * PEER_TOPK)
    top, ci = lax.top_k(cand, PEER_TOPK)
    e1 = jnp.take_along_axis(i1, ci // PEER_TOPK, axis=-1)
    e2 = jnp.take_along_axis(i2, ci % PEER_TOPK, axis=-1)
    idx = (e1 * PEER_KEYS + e2).reshape(T, PEER_HEADS * PEER_TOPK)
    g = jax.nn.softmax(top, axis=-1).reshape(T, PEER_HEADS * PEER_TOPK)
    nc = T // PEER_CHUNK

    def chunk(args):
        tc, ic, gc = args
        u = expert_u[ic]
        a = jax.nn.gelu(jnp.einsum('ted,td->te', u, tc, preferred_element_type=jnp.float32), approximate=False)
        vv = expert_v[ic]
        return jnp.einsum('te,ted->td', (gc * a).astype(tc.dtype), vv)

    out = lax.map(chunk, (t.reshape(nc, PEER_CHUNK, D),
                          idx.reshape(nc, PEER_CHUNK, -1),
                          g.reshape(nc, PEER_CHUNK, -1)))
    return out.reshape(B, S, D).astype(h.dtype)


def encoder_layer(x, norm1, w_in, q_norm_a, k_norm_a, sink_a, q_norm_b, k_norm_b,
                  w_o_a, w_o_b, w_out, norm2, w_query, sub_keys_1, sub_keys_2, expert_u, expert_v):
    B, S, _ = x.shape
    h = rms_norm(x, norm1)
    proj = h @ w_in
    qa, ka, va, qb, kb, vb, ga, gb = jnp.split(proj, SPLITS, axis=-1)
    qa = qk_prep(qa.reshape(B, S, A_Q_HEADS, HEAD_DIM), q_norm_a).reshape(B, S, A_KV_HEADS, A_GROUP, HEAD_DIM)
    ka = qk_prep(ka.reshape(B, S, A_KV_HEADS, HEAD_DIM), k_norm_a)
    va = va.reshape(B, S, A_KV_HEADS, HEAD_DIM)
    oa, _ = banded_attention(qa, ka, va, A_WINDOW, sink_a)
    ya = oa.reshape(B, S, A_Q_W).astype(x.dtype) @ w_o_a
    qb = qk_prep(qb.reshape(B, S, B_HEADS, HEAD_DIM), q_norm_b)
    kb = qk_prep(kb.reshape(B, S, B_HEADS, HEAD_DIM), k_norm_b)
    vb = vb.reshape(B, S, B_HEADS, HEAD_DIM)
    outs, lses = [], []
    for gi, (win, dil) in enumerate(B_PATTERNS):
        hs = slice(gi * B_HEADS_PER_GROUP, (gi + 1) * B_HEADS_PER_GROUP)
        o, l = dilated_attention(qb[:, :, hs], kb[:, :, hs], vb[:, :, hs], win, dil)
        outs.append(o)
        lses.append(l)
    wts = jax.nn.softmax(jnp.stack(lses, axis=0), axis=0)
    ob = jnp.sum(wts[..., None] * jnp.stack(outs, axis=0), axis=0)
    yb = ob.reshape(B, S, B_OUT_W).astype(x.dtype) @ w_o_b
    merged = jax.nn.sigmoid(ga) * ya + jax.nn.sigmoid(gb) * yb
    x = x + merged @ w_out
    x = x + peer(rms_norm(x, norm2), w_query, sub_keys_1, sub_keys_2, expert_u, expert_v)
    return x


def run_trunk(x, norm1, w_in, q_norm_a, k_norm_a, sink_a, q_norm_b, k_norm_b,
              w_o_a, w_o_b, w_out, norm2, w_query, sub_keys_1, sub_keys_2, expert_u, expert_v):
    for l in range(DEPTH):
        x = encoder_layer(x, norm1[l], w_in[l], q_norm_a[l], k_norm_a[l], sink_a[l], q_norm_b[l], k_norm_b[l],
                          w_o_a[l], w_o_b[l], w_out[l], norm2[l], w_query[l], sub_keys_1[l], sub_keys_2[l],
                          expert_u[l], expert_v[l])
    return x


def setup_inputs(seed: int = 0) -> dict:
    key = jax.random.key(seed)
    ks = jax.random.split(key, 20)
    f32 = jnp.float32
    nrm = lambda k, shape, scale: jax.random.normal(k, shape, f32) * scale
    gain = lambda k, shape: 1.0 + 0.02 * jax.random.normal(k, shape, f32)
    return {
        "x_prompt": nrm(ks[0], (BATCH, SEQ, D_MODEL), 1.0),
        "x_sample": nrm(ks[1], (DEC_BATCH, DEC_SEQ, D_MODEL), 1.0),
        "norm1": gain(ks[2], (DEPTH, D_MODEL)),
        "w_in": nrm(ks[3], (DEPTH, D_MODEL, IN_COLS), D_MODEL ** -0.5),
        "q_norm_a": gain(ks[4], (DEPTH, HEAD_DIM)),
        "k_norm_a": gain(ks[5], (DEPTH, HEAD_DIM)),
        "sink_a": nrm(ks[6], (DEPTH, A_Q_HEADS), 0.5),
        "q_norm_b": gain(ks[7], (DEPTH, HEAD_DIM)),
        "k_norm_b": gain(ks[8], (DEPTH, HEAD_DIM)),
        "w_o_a": nrm(ks[9], (DEPTH, A_Q_W, D_MODEL), A_Q_W ** -0.5),
        "w_o_b": nrm(ks[10], (DEPTH, B_OUT_W, D_MODEL), B_OUT_W ** -0.5),
        "w_out": nrm(ks[11], (DEPTH, D_MODEL, D_MODEL), D_MODEL ** -0.5),
        "norm2": gain(ks[12], (DEPTH, D_MODEL)),
        "w_query": nrm(ks[13], (DEPTH, D_MODEL, PEER_HEADS * 2 * PEER_HALF), D_MODEL ** -0.5),
        "sub_keys_1": nrm(ks[14], (DEPTH, PEER_HEADS, PEER_KEYS, PEER_HALF), PEER_HALF ** -0.5),
        "sub_keys_2": nrm(ks[15], (DEPTH, PEER_HEADS, PEER_KEYS, PEER_HALF), PEER_HALF ** -0.5),
        "expert_u": nrm(ks[16], (DEPTH, PEER_EXPERTS, D_MODEL), D_MODEL ** -0.5),
        "expert_v": nrm(ks[17], (DEPTH, PEER_EXPERTS, D_MODEL), (PEER_HEADS * PEER_TOPK) ** -0.5),
    }


def reference(x_prompt, x_sample, norm1, w_in, q_norm_a, k_norm_a, sink_a, q_norm_b, k_norm_b,
              w_o_a, w_o_b, w_out, norm2, w_query, sub_keys_1, sub_keys_2, expert_u, expert_v):
    y_prompt = run_trunk(x_prompt, norm1, w_in, q_norm_a, k_norm_a, sink_a, q_norm_b, k_norm_b,
                         w_o_a, w_o_b, w_out, norm2, w_query, sub_keys_1, sub_keys_2, expert_u, expert_v)
    y_sample = run_trunk(x_sample, norm1, w_in, q_norm_a, k_norm_a, sink_a, q_norm_b, k_norm_b,
                         w_o_a, w_o_b, w_out, norm2, w_query, sub_keys_1, sub_keys_2, expert_u, expert_v)
    return (y_prompt, y_sample)
```

```python
import functools

import numpy as np
import jax
import jax.numpy as jnp
from jax import lax
from jax.experimental import pallas as pl
from jax.experimental.pallas import tpu as pltpu

F32 = jnp.float32
BF16 = jnp.bfloat16
I32 = jnp.int32

D_MODEL = 1024
HEAD_DIM = 64
ROPE_DIM = HEAD_DIM // 4
ROPE_HALF = ROPE_DIM // 2
ROPE_THETA = 500000.0
EPS = 1e-6
A_Q_HEADS = 8
A_KV_HEADS = 2
A_GROUP = A_Q_HEADS // A_KV_HEADS
A_WINDOW = 128
B_PATTERNS = ((128, 1), (512, 4), (2048, 16))
B_HEADS_PER_GROUP = 4
B_GROUPS = len(B_PATTERNS)
A_Q_W = A_Q_HEADS * HEAD_DIM
A_KV_W = A_KV_HEADS * HEAD_DIM
B_GROUP_W = B_HEADS_PER_GROUP * HEAD_DIM
B_W = B_GROUPS * B_GROUP_W
IN_COLS = A_Q_W + 2 * A_KV_W + 3 * B_W + 2 * D_MODEL
PEER_HEADS = 8
PEER_KEYS = 128
PEER_EXPERTS = PEER_KEYS * PEER_KEYS
PEER_HALF = 128
PEER_TOPK = 16
PEER_SLOTS = PEER_HEADS * PEER_TOPK
PEER_Q_W = PEER_HEADS * 2 * PEER_HALF

LANES = 128
MASK_VALUE = -1e30
NEG = float(np.finfo(np.float32).min)
VMEM_LIMIT = 56 * 1024 * 1024

OFF_QA = 0
OFF_KA = OFF_QA + A_Q_W
OFF_VA = OFF_KA + A_KV_W
OFF_QB = OFF_VA + A_KV_W
OFF_KB = OFF_QB + B_W
OFF_VB = OFF_KB + B_W
OFF_GA = OFF_VB + B_W
OFF_GB = OFF_GA + D_MODEL

TM_IN = 256
TM_POST = 256
TM_TOPK = 128
TM_U = 512
EC_U = 512
TM_V = 256
EC_V = 1024
ATTN_TQ = 256

def _dot(a, b):
    return jnp.dot(a, b, preferred_element_type=F32)


def _dot_nt(a, b):
    return lax.dot_general(a, b, (((1,), (1,)), ((), ())), preferred_element_type=F32)


def _rms(x, g):
    ms = jnp.mean(x * x, axis=-1, keepdims=True)
    return (x * lax.rsqrt(ms + EPS)) * g


def _head_prep(t, gain, cosv, sinv, bd, lane_lo):
    sq = t * t
    hi = sq.astype(BF16)
    lo = (sq - hi.astype(F32)).astype(BF16)
    ms = _dot(hi, bd) + _dot(lo, bd)
    tn = (t * lax.rsqrt(ms + EPS)) * gain
    partner = jnp.where(lane_lo, pltpu.roll(tn, LANES - ROPE_HALF, 1), pltpu.roll(tn, ROPE_HALF, 1))
    return tn * cosv + partner * sinv


def _inproj_body(x_ref, n1_ref, w_ref, gqa_ref, gka_ref, gqb_ref, gkb_ref, cos_ref, sin_ref, bd_ref,
                 qa_ref, ka_ref, va_ref, qb_ref, kb_ref, vb_ref, gate_ref):
    h = _rms(x_ref[...], n1_ref[...]).astype(BF16)
    cosv = cos_ref[...]
    sinv = sin_ref[...]
    bd = bd_ref[...]
    lane = lax.broadcasted_iota(I32, (1, LANES), 1)
    lane_lo = (lane % HEAD_DIM) < ROPE_HALF

    def prepped(off, width, gain_ref, scale, out_ref):
        for c in range(width // LANES):
            t = _dot(h, w_ref[:, off + c * LANES: off + (c + 1) * LANES])
            t = _head_prep(t, gain_ref[...], cosv, sinv, bd, lane_lo)
            if scale != 1.0:
                t = t * scale
            out_ref[:, c * LANES:(c + 1) * LANES] = t.astype(out_ref.dtype)

    qscale = HEAD_DIM ** -0.5
    prepped(OFF_QA, A_Q_W, gqa_ref, qscale, qa_ref)
    prepped(OFF_KA, A_KV_W, gka_ref, 1.0, ka_ref)
    va_ref[...] = _dot(h, w_ref[:, OFF_VA:OFF_VA + A_KV_W]).astype(va_ref.dtype)
    prepped(OFF_QB, B_W, gqb_ref, qscale, qb_ref)
    prepped(OFF_KB, B_W, gkb_ref, 1.0, kb_ref)
    vb_ref[...] = _dot(h, w_ref[:, OFF_VB:OFF_VB + B_W]).astype(vb_ref.dtype)
    for c in range(2 * D_MODEL // 512):
        g = _dot(h, w_ref[:, OFF_GA + c * 512: OFF_GA + (c + 1) * 512])
        gate_ref[:, c * 512:(c + 1) * 512] = jax.nn.sigmoid(g)


def _inproj(x2d, seq, n1, w_in, gqa, gka, gqb, gkb, cos_t, sin_t, bd):
    tokens = x2d.shape[0]
    tm = TM_IN
    blocks_per_seq = seq // tm
    row = lambda i: (i, 0)
    const = lambda i: (0, 0)
    pos = lambda i: (i % blocks_per_seq, 0)
    out_shape = (
        jax.ShapeDtypeStruct((tokens, A_Q_W), BF16),
        jax.ShapeDtypeStruct((tokens, A_KV_W), BF16),
        jax.ShapeDtypeStruct((tokens, A_KV_W), BF16),
        jax.ShapeDtypeStruct((tokens, B_W), BF16),
        jax.ShapeDtypeStruct((tokens, B_W), BF16),
        jax.ShapeDtypeStruct((tokens, B_W), BF16),
        jax.ShapeDtypeStruct((tokens, 2 * D_MODEL), F32),
    )
    return pl.pallas_call(
        _inproj_body,
        out_shape=out_shape,
        grid=(tokens // tm,),
        in_specs=[
            pl.BlockSpec((tm, D_MODEL), row),
            pl.BlockSpec((1, D_MODEL), const),
            pl.BlockSpec((D_MODEL, IN_COLS), const),
            pl.BlockSpec((1, LANES), const),
            pl.BlockSpec((1, LANES), const),
            pl.BlockSpec((1, LANES), const),
            pl.BlockSpec((1, LANES), const),
            pl.BlockSpec((tm, LANES), pos),
            pl.BlockSpec((tm, LANES), pos),
            pl.BlockSpec((LANES, LANES), const),
        ],
        out_specs=(
            pl.BlockSpec((tm, A_Q_W), row),
            pl.BlockSpec((tm, A_KV_W), row),
            pl.BlockSpec((tm, A_KV_W), row),
            pl.BlockSpec((tm, B_W), row),
            pl.BlockSpec((tm, B_W), row),
            pl.BlockSpec((tm, B_W), row),
            pl.BlockSpec((tm, 2 * D_MODEL), row),
        ),
        compiler_params=pltpu.CompilerParams(
            dimension_semantics=("arbitrary",), vmem_limit_bytes=VMEM_LIMIT),
        name="inproj",
    )(x2d, n1, w_in, gqa, gka, gqb, gkb, cos_t, sin_t, bd)


def _attn_body(*refs, window, tq, kw, length, nkv, grp, has_sink, want_lse):
    refs = list(refs)
    sink_ref = refs.pop(0) if has_sink else None
    q_ref, k_ref, v_ref, o_ref = refs[:4]
    lse_ref = refs[4] if want_lse else None

    q0 = pl.program_id(2) * tq
    if kw == length:
        start = 0
        kwin = k_ref[...]
        vwin = v_ref[...]
    else:
        start = pl.multiple_of(jnp.clip(q0 - window, 0, length - kw), window)
        kwin = k_ref[pl.ds(start, kw), :]
        vwin = v_ref[pl.ds(start, kw), :]
    q = q_ref[...]

    rows = grp * tq
    qi = lax.broadcasted_iota(I32, (rows, kw), 0) & (tq - 1)
    kj = lax.broadcasted_iota(I32, (rows, kw), 1)
    valid = jnp.abs(kj - qi + (start - q0)) <= window

    o_parts = [None] * (nkv * grp)
    lse_parts = [None] * (nkv * grp)
    for j in range(nkv):
        kh = kwin[:, j * HEAD_DIM:(j + 1) * HEAD_DIM]
        vh = vwin[:, j * HEAD_DIM:(j + 1) * HEAD_DIM]
        heads = [j * grp + g for g in range(grp)]
        qh = [q[:, h * HEAD_DIM:(h + 1) * HEAD_DIM] for h in heads]
        qh = qh[0] if grp == 1 else jnp.concatenate(qh, axis=0)
        s = jnp.where(valid, _dot_nt(qh, kh), MASK_VALUE)
        m = jnp.max(s, axis=-1, keepdims=True)
        if has_sink:
            sk = [jnp.full((tq, 1), sink_ref[h], F32) for h in heads]
            sk = sk[0] if grp == 1 else jnp.concatenate(sk, axis=0)
            m = jnp.maximum(m, sk)
        p = jnp.exp(s - m)
        denom = jnp.sum(p, axis=-1, keepdims=True)
        if has_sink:
            denom = denom + jnp.exp(sk - m)
        o = _dot(p.astype(BF16), vh) / denom
        lse = m + jnp.log(denom) if want_lse else None
        for g, h in enumerate(heads):
            o_parts[h] = o[g * tq:(g + 1) * tq]
            if want_lse:
                lse_parts[h] = jnp.broadcast_to(lse[g * tq:(g + 1) * tq], (tq, HEAD_DIM))
    o_ref[...] = jnp.concatenate(o_parts, axis=-1).astype(o_ref.dtype)
    if want_lse:
        lse_ref[...] = jnp.concatenate(lse_parts, axis=-1)


def _banded_attention(q, k, v, *, window, dil, q_block, kv_block, nkv, grp, sink=None, want_lse=False,
                      out_dtype=BF16, name):
    batch, length, _ = q.shape
    wq = nkv * grp * HEAD_DIM
    wk = nkv * HEAD_DIM
    tq = min(ATTN_TQ, length)
    kw = min(length, tq + 2 * window)
    out_cols = dil * wq
    body = functools.partial(_attn_body, window=window, tq=tq, kw=kw, length=length, nkv=nkv, grp=grp,
                             has_sink=sink is not None, want_lse=want_lse)
    in_specs = [
        pl.BlockSpec((None, tq, wq), lambda b, r, t: (b, t, q_block(r))),
        pl.BlockSpec((None, length, wk), lambda b, r, t: (b, 0, kv_block(r))),
        pl.BlockSpec((None, length, wk), lambda b, r, t: (b, 0, kv_block(r))),
    ]
    args = [q, k, v]
    if sink is not None:
        in_specs.insert(0, pl.BlockSpec(memory_space=pltpu.SMEM))
        args.insert(0, sink)
    o_spec = pl.BlockSpec((None, tq, wq), lambda b, r, t: (b, t, r))
    out_shape = [jax.ShapeDtypeStruct((batch, length, out_cols), out_dtype)]
    out_specs = [o_spec]
    if want_lse:
        out_shape.append(jax.ShapeDtypeStruct((batch, length, out_cols), F32))
        out_specs.append(o_spec)
    return pl.pallas_call(
        body,
        out_shape=tuple(out_shape),
        grid=(batch, dil, length // tq),
        in_specs=in_specs,
        out_specs=tuple(out_specs),
        compiler_params=pltpu.CompilerParams(
            dimension_semantics=("arbitrary", "arbitrary", "arbitrary"), vmem_limit_bytes=VMEM_LIMIT),
        name=name,
    )(*args)


def _post_body(oa_ref, o1_ref, o2_ref, o3_ref, l1_ref, l2_ref, l3_ref, ga_ref, gb_ref, x_ref,
               woa_ref, wob_ref, wout_ref, n2_ref, wq_ref,
               x1_ref, h2_ref, qp_ref):
    l1, l2, l3 = l1_ref[...], l2_ref[...], l3_ref[...]
    mx = jnp.maximum(jnp.maximum(l1, l2), l3)
    e1, e2, e3 = jnp.exp(l1 - mx), jnp.exp(l2 - mx), jnp.exp(l3 - mx)
    tot = e1 + e2 + e3
    ob = (e1 / tot) * o1_ref[...] + (e2 / tot) * o2_ref[...] + (e3 / tot) * o3_ref[...]
    ya = _dot(oa_ref[...], woa_ref[...])
    yb = _dot(ob.astype(BF16), wob_ref[...])
    merged = ga_ref[...] * ya + gb_ref[...] * yb
    x1 = x_ref[...] + _dot(merged.astype(BF16), wout_ref[...])
    x1_ref[...] = x1
    h2 = _rms(x1, n2_ref[...]).astype(BF16)
    h2_ref[...] = h2
    for c in range(PEER_Q_W // PEER_HALF):
        qp_ref[c] = _dot(h2, wq_ref[:, c * PEER_HALF:(c + 1) * PEER_HALF]).astype(qp_ref.dtype)


def _post(oa, o_groups, lse_groups, gates, x2d, w_o_a, w_o_b, w_out, n2, w_query):
    tokens = x2d.shape[0]
    tm = TM_POST
    row = lambda i: (i, 0)
    const = lambda i: (0, 0)
    nq = PEER_Q_W // PEER_HALF
    return pl.pallas_call(
        _post_body,
        out_shape=(
            jax.ShapeDtypeStruct((tokens, D_MODEL), F32),
            jax.ShapeDtypeStruct((tokens, D_MODEL), BF16),
            jax.ShapeDtypeStruct((nq, tokens, PEER_HALF), BF16),
        ),
        grid=(tokens // tm,),
        in_specs=[pl.BlockSpec((tm, A_Q_W), row)]
        + [pl.BlockSpec((tm, B_GROUP_W), row)] * (2 * B_GROUPS)
        + [
            pl.BlockSpec((tm, D_MODEL), lambda i: (i, 0)),
            pl.BlockSpec((tm, D_MODEL), lambda i: (i, 1)),
            pl.BlockSpec((tm, D_MODEL), row),
            pl.BlockSpec((A_Q_W, D_MODEL), const),
            pl.BlockSpec((B_GROUP_W, D_MODEL), const),
            pl.BlockSpec((D_MODEL, D_MODEL), const),
            pl.BlockSpec((1, D_MODEL), const),
            pl.BlockSpec((D_MODEL, PEER_Q_W), const),
        ],
        out_specs=(
            pl.BlockSpec((tm, D_MODEL), row),
            pl.BlockSpec((tm, D_MODEL), row),
            pl.BlockSpec((nq, tm, PEER_HALF), lambda i: (0, i, 0)),
        ),
        compiler_params=pltpu.CompilerParams(
            dimension_semantics=("arbitrary",), vmem_limit_bytes=VMEM_LIMIT),
        name="post",
    )(oa, *o_groups, *lse_groups, gates, gates, x2d, w_o_a, w_o_b, w_out, n2, w_query)


def _top_rows(s, k):
    n, tm = s.shape
    riota = lax.broadcasted_iota(I32, s.shape, 0).astype(F32)
    kiota = lax.broadcasted_iota(I32, (k, tm), 0)
    vals = jnp.zeros((k, tm), F32)
    rows = jnp.zeros((k, tm), F32)
    for r in range(k):
        m = jnp.max(s, axis=0, keepdims=True)
        pos = jnp.min(jnp.where(s == m, riota, float(n)), axis=0, keepdims=True)
        s = jnp.where(riota == pos, NEG, s)
        vals = jnp.where(kiota == r, m, vals)
        rows = jnp.where(kiota == r, pos, rows)
    return vals, rows


def _candidates(v1, i1, v2, i2):
    tm = v1.shape[1]
    sub = 8
    siota = lax.broadcasted_iota(I32, (sub, tm), 0)
    cand = [v1[0:1] + v2, v1[1:2] + v2[0:sub]]
    c1 = [jnp.broadcast_to(i1[0:1], (PEER_TOPK, tm)), jnp.broadcast_to(i1[1:2], (sub, tm))]
    c2 = [i2, i2[0:sub]]
    for i in range(2, sub):
        cand.append(jnp.where(siota < PEER_TOPK // (i + 1), v1[i:i + 1] + v2[0:sub], NEG))
        c1.append(jnp.broadcast_to(i1[i:i + 1], (sub, tm)))
        c2.append(i2[0:sub])
    cand.append(v1[sub:] + v2[0:1])
    c1.append(i1[sub:])
    c2.append(jnp.broadcast_to(i2[0:1], (PEER_TOPK - sub, tm)))
    return jnp.concatenate(cand, axis=0), jnp.concatenate(c1, axis=0), jnp.concatenate(c2, axis=0)


def _topk_body(qp_ref, k1_ref, k2_ref, e1_ref, e2_ref, g_ref, e1t_scr, e2t_scr, gt_scr):
    tm = qp_ref.shape[1]

    def head(h, carry):
        s1 = _dot_nt(k1_ref[h], qp_ref[2 * h])
        s2 = _dot_nt(k2_ref[h], qp_ref[2 * h + 1])
        v1, i1 = _top_rows(s1, PEER_TOPK)
        v2, i2 = _top_rows(s2, PEER_TOPK)
        cand, c1, c2 = _candidates(v1, i1, v2, i2)
        n = cand.shape[0]
        riota = lax.broadcasted_iota(I32, cand.shape, 0).astype(F32)
        kiota = lax.broadcasted_iota(I32, (PEER_TOPK, tm), 0)
        top = jnp.zeros((PEER_TOPK, tm), F32)
        e1 = jnp.zeros((PEER_TOPK, tm), F32)
        e2 = jnp.zeros((PEER_TOPK, tm), F32)
        for r in range(PEER_TOPK):
            m = jnp.max(cand, axis=0, keepdims=True)
            pos = jnp.min(jnp.where(cand == m, riota, float(n)), axis=0, keepdims=True)
            sel = riota == pos
            cand = jnp.where(sel, NEG, cand)
            top = jnp.where(kiota == r, m, top)
            e1 = jnp.where(kiota == r, jnp.max(jnp.where(sel, c1, -1.0), axis=0, keepdims=True), e1)
            e2 = jnp.where(kiota == r, jnp.max(jnp.where(sel, c2, -1.0), axis=0, keepdims=True), e2)
        ex = jnp.exp(top - top[0:1])
        gate = ex / jnp.sum(ex, axis=0, keepdims=True)
        base = pl.multiple_of(h * PEER_TOPK, PEER_TOPK)
        e1t_scr[pl.ds(base, PEER_TOPK), :] = e1
        e2t_scr[pl.ds(base, PEER_TOPK), :] = e2
        gt_scr[pl.ds(base, PEER_TOPK), :] = gate
        return carry

    lax.fori_loop(0, PEER_HEADS, head, 0)
    e1_ref[...] = e1t_scr[...].T.astype(I32)
    e2_ref[...] = e2t_scr[...].T.astype(I32)
    g_ref[...] = gt_scr[...].T


def _topk(qp, k1, k2):
    nq, tokens, _ = qp.shape
    tm = TM_TOPK
    assert tm == PEER_SLOTS
    row = lambda i: (i, 0)
    const3 = lambda i: (0, 0, 0)
    return pl.pallas_call(
        _topk_body,
        out_shape=(
            jax.ShapeDtypeStruct((tokens, PEER_SLOTS), I32),
            jax.ShapeDtypeStruct((tokens, PEER_SLOTS), I32),
            jax.ShapeDtypeStruct((tokens, PEER_SLOTS), F32),
        ),
        grid=(tokens // tm,),
        in_specs=[
            pl.BlockSpec((nq, tm, PEER_HALF), lambda i: (0, i, 0)),
            pl.BlockSpec((PEER_HEADS, PEER_KEYS, PEER_HALF), const3),
            pl.BlockSpec((PEER_HEADS, PEER_KEYS, PEER_HALF), const3),
        ],
        out_specs=(
            pl.BlockSpec((tm, PEER_SLOTS), row),
            pl.BlockSpec((tm, PEER_SLOTS), row),
            pl.BlockSpec((tm, PEER_SLOTS), row),
        ),
        scratch_shapes=[
            pltpu.VMEM((PEER_SLOTS, tm), F32),
            pltpu.VMEM((PEER_SLOTS, tm), F32),
            pltpu.VMEM((PEER_SLOTS, tm), F32),
        ],
        compiler_params=pltpu.CompilerParams(
            dimension_semantics=("arbitrary",), vmem_limit_bytes=VMEM_LIMIT),
        name="topk",
    )(qp, k1, k2)


def _peer_u_body(h2_ref, ut_ref, e1_ref, e2_ref, g_ref, w_ref, acc_ref):
    j = pl.program_id(1)

    @pl.when(j == 0)
    def _():
        acc_ref[...] = jnp.zeros_like(acc_ref)

    scores = _dot(h2_ref[...], ut_ref[...])
    e1 = e1_ref[...]
    e2 = e2_ref[...]
    acc = acc_ref[...]
    for c in range(EC_U // PEER_KEYS):
        picked = jnp.take_along_axis(scores[:, c * PEER_KEYS:(c + 1) * PEER_KEYS], e2, axis=1)
        acc = jnp.where(e1 == j * (EC_U // PEER_KEYS) + c, picked, acc)
    acc_ref[...] = acc

    @pl.when(j == pl.num_programs(1) - 1)
    def _():
        w_ref[...] = g_ref[...] * (0.5 * acc * (1.0 + lax.erf(acc * np.float32(np.sqrt(0.5)))))


def _peer_u(h2, u_t, e1, e2, gate):
    tokens = h2.shape[0]
    tm = TM_U
    row = lambda i, j: (i, 0)
    return pl.pallas_call(
        _peer_u_body,
        out_shape=jax.ShapeDtypeStruct((tokens, PEER_SLOTS), F32),
        grid=(tokens // tm, PEER_EXPERTS // EC_U),
        in_specs=[
            pl.BlockSpec((tm, D_MODEL), row),
            pl.BlockSpec((D_MODEL, EC_U), lambda i, j: (0, j)),
            pl.BlockSpec((tm, PEER_SLOTS), row),
            pl.BlockSpec((tm, PEER_SLOTS), row),
            pl.BlockSpec((tm, PEER_SLOTS), row),
        ],
        out_specs=pl.BlockSpec((tm, PEER_SLOTS), row),
        scratch_shapes=[pltpu.VMEM((tm, PEER_SLOTS), F32)],
        compiler_params=pltpu.CompilerParams(
            dimension_semantics=("arbitrary", "arbitrary"), vmem_limit_bytes=VMEM_LIMIT),
        name="peer_u",
    )(h2, u_t, e1, e2, gate)


def _peer_v_body(e1_ref, e2_ref, w_ref, v_ref, x1_ref, y_ref, wd_scr, acc_scr):
    tm = e1_ref.shape[0]
    j = pl.program_id(1)

    @pl.when(j == 0)
    def _():
        eiota = lax.broadcasted_iota(I32, (PEER_KEYS, PEER_SLOTS), 0)

        def token(t, carry):
            e1 = e1_ref[pl.ds(t, 1), :]
            e2 = e2_ref[pl.ds(t, 1), :]
            wt = w_ref[pl.ds(t, 1), :]
            a = jnp.where(eiota == e1, wt, 0.0).astype(BF16)
            b = jnp.where(eiota == e2, 1.0, 0.0).astype(BF16)
            wd_scr[pl.ds(pl.multiple_of(t * PEER_KEYS, PEER_KEYS), PEER_KEYS), :] = _dot_nt(a, b)
            return carry

        lax.fori_loop(0, tm, token, 0)
        acc_scr[...] = jnp.zeros_like(acc_scr)

    n_e1 = EC_V // PEER_KEYS
    parts = [wd_scr[pl.ds(j * n_e1 + c, tm, stride=PEER_KEYS), :] for c in range(n_e1)]
    dense_w = jnp.concatenate(parts, axis=-1).astype(BF16)
    acc_scr[...] += _dot(dense_w, v_ref[...])

    @pl.when(j == pl.num_programs(1) - 1)
    def _():
        y_ref[...] = x1_ref[...] + acc_scr[...]


def _peer_v(e1, e2, w, v_bf16, x1):
    tokens = x1.shape[0]
    tm = TM_V
    row = lambda i, j: (i, 0)
    return pl.pallas_call(
        _peer_v_body,
        out_shape=jax.ShapeDtypeStruct((tokens, D_MODEL), F32),
        grid=(tokens // tm, PEER_EXPERTS // EC_V),
        in_specs=[
            pl.BlockSpec((tm, PEER_SLOTS), row),
            pl.BlockSpec((tm, PEER_SLOTS), row),
            pl.BlockSpec((tm, PEER_SLOTS), row),
            pl.BlockSpec((EC_V, D_MODEL), lambda i, j: (j, 0)),
            pl.BlockSpec((tm, D_MODEL), row),
        ],
        out_specs=pl.BlockSpec((tm, D_MODEL), row),
        scratch_shapes=[
            pltpu.VMEM((tm * PEER_KEYS, PEER_KEYS), F32),
            pltpu.VMEM((tm, D_MODEL), F32),
        ],
        compiler_params=pltpu.CompilerParams(
            dimension_semantics=("arbitrary", "arbitrary"), vmem_limit_bytes=VMEM_LIMIT),
        name="peer_v",
    )(e1, e2, w, v_bf16, x1)


def _rope_tables(seq):
    inv = ROPE_THETA ** (-jnp.arange(0, ROPE_DIM, 2, dtype=F32) / ROPE_DIM)
    ang = jnp.arange(seq, dtype=F32)[:, None] * inv[None, :]
    cos = jnp.cos(ang)
    sin = jnp.sin(ang)
    lane = np.arange(LANES)
    in_head = lane % HEAD_DIM
    freq = in_head % ROPE_HALF
    cos_l = jnp.where(in_head < ROPE_DIM, cos[:, freq], 1.0)
    sin_l = jnp.where(in_head < ROPE_HALF, -sin[:, freq], jnp.where(in_head < ROPE_DIM, sin[:, freq], 0.0))
    return cos_l.astype(F32), sin_l.astype(F32)


def _layer(x, p):
    batch, seq, _ = x.shape
    tokens = batch * seq
    x2d = x.reshape(tokens, D_MODEL)
    cos_t, sin_t = _rope_tables(seq)
    qa, ka, va, qb, kb, vb, gates = _inproj(
        x2d, seq, p["norm1"], p["w_in"], p["gqa"], p["gka"], p["gqb"], p["gkb"], cos_t, sin_t, p["bd"])

    oa, = _banded_attention(
        qa.reshape(batch, seq, A_Q_W), ka.reshape(batch, seq, A_KV_W), va.reshape(batch, seq, A_KV_W),
        window=A_WINDOW, dil=1, q_block=lambda r: 0, kv_block=lambda r: 0, nkv=A_KV_HEADS, grp=A_GROUP,
        sink=p["sink"], name="attn_a")

    o_groups, lse_groups = [], []
    for gi, (win, dil) in enumerate(B_PATTERNS):
        sub = seq // dil
        blk = functools.partial(lambda r, gi: r * B_GROUPS + gi, gi=gi)
        o, lse = _banded_attention(
            qb.reshape(batch, sub, dil * B_W), kb.reshape(batch, sub, dil * B_W), vb.reshape(batch, sub, dil * B_W),
            window=win // (2 * dil), dil=dil, q_block=blk, kv_block=blk, nkv=B_HEADS_PER_GROUP, grp=1,
            want_lse=True, out_dtype=F32, name=f"attn_b{gi}")
        o_groups.append(o.reshape(tokens, B_GROUP_W))
        lse_groups.append(lse.reshape(tokens, B_GROUP_W))

    x1, h2, qp = _post(oa.reshape(tokens, A_Q_W), o_groups, lse_groups, gates, x2d,
                       p["w_o_a"], p["w_o_b"], p["w_out"], p["norm2"], p["w_query"])
    e1, e2, gate = _topk(qp, p["k1"], p["k2"])
    w = _peer_u(h2, p["u_t"], e1, e2, gate)
    y = _peer_v(e1, e2, w, p["v"], x1)
    return y.reshape(batch, seq, D_MODEL)


def _tile_gain(g):
    return jnp.tile(g.astype(F32), LANES // HEAD_DIM).reshape(1, LANES)


def kernel(x_prompt, x_sample, norm1, w_in, q_norm_a, k_norm_a, sink_a, q_norm_b, k_norm_b, w_o_a, w_o_b, w_out,
           norm2, w_query, sub_keys_1, sub_keys_2, expert_u, expert_v):
    depth = norm1.shape[0]
    head_of_lane = np.arange(LANES) // HEAD_DIM
    bd = jnp.asarray((head_of_lane[:, None] == head_of_lane[None, :]) / HEAD_DIM, dtype=BF16)
    params = []
    for l in range(depth):
        params.append({
            "norm1": norm1[l].reshape(1, D_MODEL),
            "w_in": w_in[l].astype(BF16),
            "gqa": _tile_gain(q_norm_a[l]), "gka": _tile_gain(k_norm_a[l]),
            "gqb": _tile_gain(q_norm_b[l]), "gkb": _tile_gain(k_norm_b[l]),
            "sink": sink_a[l].astype(F32),
            "w_o_a": w_o_a[l].astype(BF16), "w_o_b": w_o_b[l].astype(BF16), "w_out": w_out[l].astype(BF16),
            "norm2": norm2[l].reshape(1, D_MODEL),
            "w_query": w_query[l].astype(BF16),
            "k1": sub_keys_1[l].astype(BF16), "k2": sub_keys_2[l].astype(BF16),
            "u_t": expert_u[l].astype(BF16).T,
            "v": expert_v[l].astype(BF16),
            "bd": bd,
        })
    ys = []
    for x in (x_prompt, x_sample):
        for p in params:
            x = _layer(x, p)
        ys.append(x)
    return tuple(ys)
```

```python
import functools

import numpy as np
import jax
import jax.numpy as jnp
from jax import lax
from jax.experimental import pallas as pl
from jax.experimental.pallas import tpu as pltpu

F32 = jnp.float32
BF16 = jnp.bfloat16
I32 = jnp.int32

D_MODEL = 1024
HEAD_DIM = 64
ROPE_DIM = HEAD_DIM // 4
ROPE_HALF = ROPE_DIM // 2
ROPE_THETA = 500000.0
EPS = 1e-6
A_Q_HEADS = 8
A_KV_HEADS = 2
A_GROUP = A_Q_HEADS // A_KV_HEADS
A_WINDOW = 128
B_PATTERNS = ((128, 1), (512, 4), (2048, 16))
B_HEADS_PER_GROUP = 4
B_GROUPS = len(B_PATTERNS)
A_Q_W = A_Q_HEADS * HEAD_DIM
A_KV_W = A_KV_HEADS * HEAD_DIM
B_GROUP_W = B_HEADS_PER_GROUP * HEAD_DIM
B_W = B_GROUPS * B_GROUP_W
IN_COLS = A_Q_W + 2 * A_KV_W + 3 * B_W + 2 * D_MODEL
PEER_HEADS = 8
PEER_KEYS = 128
PEER_EXPERTS = PEER_KEYS * PEER_KEYS
PEER_HALF = 128
PEER_TOPK = 16
PEER_SLOTS = PEER_HEADS * PEER_TOPK
PEER_Q_W = PEER_HEADS * 2 * PEER_HALF

LANES = 128
SUBLANES = 8
MASK_VALUE = -1e30
NEG = float(np.finfo(np.float32).min)
VMEM_LIMIT = 56 * 1024 * 1024

OFF_QA = 0
OFF_KA = OFF_QA + A_Q_W
OFF_VA = OFF_KA + A_KV_W
OFF_QB = OFF_VA + A_KV_W
OFF_KB = OFF_QB + B_W
OFF_VB = OFF_KB + B_W
OFF_GA = OFF_VB + B_W
OFF_GB = OFF_GA + D_MODEL

TM_IN = 256
TM_POST = 256
TM_TOPK = 128
TOPK_HEADS_PER_ITER = 4
TM_U = 2048
EC_U = 512
TM_V = 512
EC_V = 1024
ATTN_TQ = 256

def _dot(a, b):
    return jnp.dot(a, b, preferred_element_type=F32)


def _dot_nt(a, b):
    return lax.dot_general(a, b, (((1,), (1,)), ((), ())), preferred_element_type=F32)


def _rms(x, g):
    ms = jnp.mean(x * x, axis=-1, keepdims=True)
    return (x * lax.rsqrt(ms + EPS)) * g


def _head_prep(t, gain, cosv, sinv, bd, lane_lo):
    sq = t * t
    hi = sq.astype(BF16)
    lo = (sq - hi.astype(F32)).astype(BF16)
    ms = _dot(hi, bd) + _dot(lo, bd)
    tn = (t * lax.rsqrt(ms + EPS)) * gain
    partner = jnp.where(lane_lo, pltpu.roll(tn, LANES - ROPE_HALF, 1), pltpu.roll(tn, ROPE_HALF, 1))
    return tn * cosv + partner * sinv


def _inproj_body(x_ref, n1_ref, w_ref, gqa_ref, gka_ref, gqb_ref, gkb_ref, cos_ref, sin_ref, bd_ref,
                 qa_ref, ka_ref, va_ref, qb_ref, kb_ref, vb_ref, gate_ref):
    h = _rms(x_ref[...], n1_ref[...]).astype(BF16)
    cosv = cos_ref[...]
    sinv = sin_ref[...]
    bd = bd_ref[...]
    lane = lax.broadcasted_iota(I32, (1, LANES), 1)
    lane_lo = (lane % HEAD_DIM) < ROPE_HALF

    def prepped(off, width, gain_ref, scale, out_ref):
        for c in range(width // LANES):
            t = _dot(h, w_ref[:, off + c * LANES: off + (c + 1) * LANES])
            t = _head_prep(t, gain_ref[...], cosv, sinv, bd, lane_lo)
            if scale != 1.0:
                t = t * scale
            out_ref[:, c * LANES:(c + 1) * LANES] = t.astype(out_ref.dtype)

    qscale = HEAD_DIM ** -0.5
    prepped(OFF_QA, A_Q_W, gqa_ref, qscale, qa_ref)
    prepped(OFF_KA, A_KV_W, gka_ref, 1.0, ka_ref)
    va_ref[...] = _dot(h, w_ref[:, OFF_VA:OFF_VA + A_KV_W]).astype(va_ref.dtype)
    prepped(OFF_QB, B_W, gqb_ref, qscale, qb_ref)
    prepped(OFF_KB, B_W, gkb_ref, 1.0, kb_ref)
    vb_ref[...] = _dot(h, w_ref[:, OFF_VB:OFF_VB + B_W]).astype(vb_ref.dtype)
    for c in range(2 * D_MODEL // 512):
        g = _dot(h, w_ref[:, OFF_GA + c * 512: OFF_GA + (c + 1) * 512])
        gate_ref[:, c * 512:(c + 1) * 512] = jax.nn.sigmoid(g)


def _inproj(x2d, seq, n1, w_in, gqa, gka, gqb, gkb, cos_t, sin_t, bd):
    tokens = x2d.shape[0]
    tm = TM_IN
    blocks_per_seq = seq // tm
    row = lambda i: (i, 0)
    const = lambda i: (0, 0)
    pos = lambda i: (i % blocks_per_seq, 0)
    out_shape = (
        jax.ShapeDtypeStruct((tokens, A_Q_W), BF16),
        jax.ShapeDtypeStruct((tokens, A_KV_W), BF16),
        jax.ShapeDtypeStruct((tokens, A_KV_W), BF16),
        jax.ShapeDtypeStruct((tokens, B_W), BF16),
        jax.ShapeDtypeStruct((tokens, B_W), BF16),
        jax.ShapeDtypeStruct((tokens, B_W), BF16),
        jax.ShapeDtypeStruct((tokens, 2 * D_MODEL), F32),
    )
    return pl.pallas_call(
        _inproj_body,
        out_shape=out_shape,
        grid=(tokens // tm,),
        in_specs=[
            pl.BlockSpec((tm, D_MODEL), row),
            pl.BlockSpec((1, D_MODEL), const),
            pl.BlockSpec((D_MODEL, IN_COLS), const),
            pl.BlockSpec((1, LANES), const),
            pl.BlockSpec((1, LANES), const),
            pl.BlockSpec((1, LANES), const),
            pl.BlockSpec((1, LANES), const),
            pl.BlockSpec((tm, LANES), pos),
            pl.BlockSpec((tm, LANES), pos),
            pl.BlockSpec((LANES, LANES), const),
        ],
        out_specs=(
            pl.BlockSpec((tm, A_Q_W), row),
            pl.BlockSpec((tm, A_KV_W), row),
            pl.BlockSpec((tm, A_KV_W), row),
            pl.BlockSpec((tm, B_W), row),
            pl.BlockSpec((tm, B_W), row),
            pl.BlockSpec((tm, B_W), row),
            pl.BlockSpec((tm, 2 * D_MODEL), row),
        ),
        compiler_params=pltpu.CompilerParams(
            dimension_semantics=("arbitrary",), vmem_limit_bytes=VMEM_LIMIT),
        name="inproj",
    )(x2d, n1, w_in, gqa, gka, gqb, gkb, cos_t, sin_t, bd)


def _attn_body(*refs, window, tq, kw, length, nkv, grp, has_sink, want_lse):
    refs = list(refs)
    sink_ref = refs.pop(0) if has_sink else None
    q_ref, k_ref, v_ref, o_ref = refs[:4]
    lse_ref = refs[4] if want_lse else None

    q0 = pl.program_id(2) * tq
    if kw == length:
        start = 0
        kwin = k_ref[...]
        vwin = v_ref[...]
    else:
        start = pl.multiple_of(jnp.clip(q0 - window, 0, length - kw), window)
        kwin = k_ref[pl.ds(start, kw), :]
        vwin = v_ref[pl.ds(start, kw), :]
    q = q_ref[...]

    rows = grp * tq
    qi = lax.broadcasted_iota(I32, (rows, kw), 0) & (tq - 1)
    kj = lax.broadcasted_iota(I32, (rows, kw), 1)
    valid = jnp.abs(kj - qi + (start - q0)) <= window

    o_parts = [None] * (nkv * grp)
    lse_parts = [None] * (nkv * grp)
    for j in range(nkv):
        kh = kwin[:, j * HEAD_DIM:(j + 1) * HEAD_DIM]
        vh = vwin[:, j * HEAD_DIM:(j + 1) * HEAD_DIM]
        heads = [j * grp + g for g in range(grp)]
        qh = [q[:, h * HEAD_DIM:(h + 1) * HEAD_DIM] for h in heads]
        qh = qh[0] if grp == 1 else jnp.concatenate(qh, axis=0)
        s = jnp.where(valid, _dot_nt(qh, kh), MASK_VALUE)
        m = jnp.max(s, axis=-1, keepdims=True)
        if has_sink:
            sk = [jnp.full((tq, 1), sink_ref[h], F32) for h in heads]
            sk = sk[0] if grp == 1 else jnp.concatenate(sk, axis=0)
            m = jnp.maximum(m, sk)
        p = jnp.exp(s - m)
        denom = jnp.sum(p, axis=-1, keepdims=True)
        if has_sink:
            denom = denom + jnp.exp(sk - m)
        o = _dot(p.astype(BF16), vh) / denom
        lse = m + jnp.log(denom) if want_lse else None
        for g, h in enumerate(heads):
            o_parts[h] = o[g * tq:(g + 1) * tq]
            if want_lse:
                lse_parts[h] = jnp.broadcast_to(lse[g * tq:(g + 1) * tq], (tq, HEAD_DIM))
    o_ref[...] = jnp.concatenate(o_parts, axis=-1).astype(o_ref.dtype)
    if want_lse:
        lse_ref[...] = jnp.concatenate(lse_parts, axis=-1)


def _banded_attention(q, k, v, *, window, dil, q_block, kv_block, nkv, grp, sink=None, want_lse=False,
                      out_dtype=BF16, name):
    batch, length, _ = q.shape
    wq = nkv * grp * HEAD_DIM
    wk = nkv * HEAD_DIM
    tq = min(ATTN_TQ, length)
    kw = min(length, tq + 2 * window)
    out_cols = dil * wq
    body = functools.partial(_attn_body, window=window, tq=tq, kw=kw, length=length, nkv=nkv, grp=grp,
                             has_sink=sink is not None, want_lse=want_lse)
    in_specs = [
        pl.BlockSpec((None, tq, wq), lambda b, r, t: (b, t, q_block(r))),
        pl.BlockSpec((None, length, wk), lambda b, r, t: (b, 0, kv_block(r))),
        pl.BlockSpec((None, length, wk), lambda b, r, t: (b, 0, kv_block(r))),
    ]
    args = [q, k, v]
    if sink is not None:
        in_specs.insert(0, pl.BlockSpec(memory_space=pltpu.SMEM))
        args.insert(0, sink)
    o_spec = pl.BlockSpec((None, tq, wq), lambda b, r, t: (b, t, r))
    out_shape = [jax.ShapeDtypeStruct((batch, length, out_cols), out_dtype)]
    out_specs = [o_spec]
    if want_lse:
        out_shape.append(jax.ShapeDtypeStruct((batch, length, out_cols), F32))
        out_specs.append(o_spec)
    return pl.pallas_call(
        body,
        out_shape=tuple(out_shape),
        grid=(batch, dil, length // tq),
        in_specs=in_specs,
        out_specs=tuple(out_specs),
        compiler_params=pltpu.CompilerParams(
            dimension_semantics=("arbitrary", "arbitrary", "arbitrary"), vmem_limit_bytes=VMEM_LIMIT),
        name=name,
    )(*args)


def _post_body(oa_ref, o1_ref, o2_ref, o3_ref, l1_ref, l2_ref, l3_ref, ga_ref, gb_ref, x_ref,
               woa_ref, wob_ref, wout_ref, n2_ref, wq_ref,
               x1_ref, h2_ref, qp_ref):
    l1, l2, l3 = l1_ref[...], l2_ref[...], l3_ref[...]
    mx = jnp.maximum(jnp.maximum(l1, l2), l3)
    e1, e2, e3 = jnp.exp(l1 - mx), jnp.exp(l2 - mx), jnp.exp(l3 - mx)
    tot = e1 + e2 + e3
    ob = (e1 / tot) * o1_ref[...] + (e2 / tot) * o2_ref[...] + (e3 / tot) * o3_ref[...]
    ya = _dot(oa_ref[...], woa_ref[...])
    yb = _dot(ob.astype(BF16), wob_ref[...])
    merged = ga_ref[...] * ya + gb_ref[...] * yb
    x1 = x_ref[...] + _dot(merged.astype(BF16), wout_ref[...])
    x1_ref[...] = x1
    h2 = _rms(x1, n2_ref[...]).astype(BF16)
    h2_ref[...] = h2
    for c in range(PEER_Q_W // PEER_HALF):
        qp_ref[c] = _dot(h2, wq_ref[:, c * PEER_HALF:(c + 1) * PEER_HALF]).astype(qp_ref.dtype)


def _post(oa, o_groups, lse_groups, gates, x2d, w_o_a, w_o_b, w_out, n2, w_query):
    tokens = x2d.shape[0]
    tm = TM_POST
    row = lambda i: (i, 0)
    const = lambda i: (0, 0)
    nq = PEER_Q_W // PEER_HALF
    return pl.pallas_call(
        _post_body,
        out_shape=(
            jax.ShapeDtypeStruct((tokens, D_MODEL), F32),
            jax.ShapeDtypeStruct((tokens, D_MODEL), BF16),
            jax.ShapeDtypeStruct((nq, tokens, PEER_HALF), BF16),
        ),
        grid=(tokens // tm,),
        in_specs=[pl.BlockSpec((tm, A_Q_W), row)]
        + [pl.BlockSpec((tm, B_GROUP_W), row)] * (2 * B_GROUPS)
        + [
            pl.BlockSpec((tm, D_MODEL), lambda i: (i, 0)),
            pl.BlockSpec((tm, D_MODEL), lambda i: (i, 1)),
            pl.BlockSpec((tm, D_MODEL), row),
            pl.BlockSpec((A_Q_W, D_MODEL), const),
            pl.BlockSpec((B_GROUP_W, D_MODEL), const),
            pl.BlockSpec((D_MODEL, D_MODEL), const),
            pl.BlockSpec((1, D_MODEL), const),
            pl.BlockSpec((D_MODEL, PEER_Q_W), const),
        ],
        out_specs=(
            pl.BlockSpec((tm, D_MODEL), row),
            pl.BlockSpec((tm, D_MODEL), row),
            pl.BlockSpec((nq, tm, PEER_HALF), lambda i: (0, i, 0)),
        ),
        compiler_params=pltpu.CompilerParams(
            dimension_semantics=("arbitrary",), vmem_limit_bytes=VMEM_LIMIT),
        name="post",
    )(oa, *o_groups, *lse_groups, gates, gates, x2d, w_o_a, w_o_b, w_out, n2, w_query)


def _tree(xs, op):
    xs = list(xs)
    while len(xs) > 1:
        nxt = [op(xs[i], xs[i + 1]) for i in range(0, len(xs) - 1, 2)]
        if len(xs) % 2:
            nxt.append(xs[-1])
        xs = nxt
    return xs[0]


def _all_rows(blocks, op):
    r = _tree(blocks, op)
    for shift in (4, 2, 1):
        r = op(r, pltpu.roll(r, shift, 0))
    return r


def _pack(rows, sub):
    blk = rows[0]
    for i in range(1, SUBLANES):
        blk = jnp.where(sub == i, rows[i], blk)
    return blk


def _select_top(blocks, ids, payload, k):
    past = float(np.finfo(np.float32).max)
    out = []
    for _ in range(k):
        m = _all_rows(blocks, jnp.maximum)
        pos = _all_rows([jnp.where(b == m, i, past) for b, i in zip(blocks, ids)], jnp.minimum)
        sel = [i == pos for i in ids]
        picked = None
        if payload is not None:
            picked = _all_rows([jnp.where(s, p, -1.0) for s, p in zip(sel, payload)], jnp.maximum)
        blocks = [jnp.where(s, NEG, b) for s, b in zip(sel, blocks)]
        out.append((m, pos, picked))
    return out


def _candidate_rows(first, second, combine, sub):
    half = SUBLANES
    first_hi = pltpu.roll(_pack(first[half:], sub), 4, 0)
    lo = _pack(second[:half], sub)
    hi = _pack(second[half:], sub)
    w = jnp.where
    a_blocks = [first[0], first[0], first[1],
                w(sub < 5, first[2], first[5]),
                w(sub < 4, first[3], first[4]),
                w(sub < 2, first[6], w(sub < 4, first[7], first_hi)),
                first_hi]
    b_blocks = [lo, hi, lo,
                w(sub < 5, lo, pltpu.roll(lo, 5, 0)),
                w(sub < 4, lo, pltpu.roll(lo, 4, 0)),
                w(sub < 2, lo, w(sub < 4, pltpu.roll(lo, 2, 0), second[0])),
                second[0]]
    t = PEER_TOPK
    flat = [sub, half + sub, t + sub,
            w(sub < 5, 2 * t + sub, 5 * t + sub - 5),
            w(sub < 4, 3 * t + sub, 4 * t + sub - 4),
            w(sub < 2, 6 * t + sub, w(sub < 4, 7 * t + sub - 2, (half + sub - 4) * t)),
            (half + 4 + sub) * t]
    used = [None, None, None, sub < 7, sub < 7, None, sub < 4]
    flat = [f if u is None else w(u, f, -1) for f, u in zip(flat, used)]
    return [combine(a, b) for a, b in zip(a_blocks, b_blocks)], flat


def _head_experts(s1, s2):
    tm = s1.shape[1]
    sub = lax.broadcasted_iota(I32, (SUBLANES, tm), 0)
    subf = sub.astype(F32)
    key_ids = [subf + float(SUBLANES * g) for g in range(PEER_KEYS // SUBLANES)]

    def half(s):
        top = _select_top([s[SUBLANES * g:SUBLANES * (g + 1)] for g in range(PEER_KEYS // SUBLANES)], key_ids,
                          None, PEER_TOPK)
        return [t[0] for t in top], [t[1] for t in top]

    v1, i1 = half(s1)
    v2, i2 = half(s2)
    cand, flat = _candidate_rows(v1, v2, lambda a, b: a + b, sub)
    cand = [jnp.where(f >= 0, c, NEG) for c, f in zip(cand, flat)]
    expert, _ = _candidate_rows(i1, i2, lambda a, b: a * float(PEER_KEYS) + b, sub)
    unused_base = PEER_TOPK * PEER_TOPK
    ids = [jnp.where(f >= 0, f, unused_base + SUBLANES * g + sub).astype(F32) for g, f in enumerate(flat)]
    top = _select_top(cand, ids, expert, PEER_TOPK)
    ex = [jnp.exp(_pack([t[0] for t in top[o:o + SUBLANES]], sub) - top[0][0]) for o in (0, SUBLANES)]
    tot = _all_rows(ex, jnp.add)
    out = []
    for o, e in zip((0, SUBLANES), ex):
        code = _pack([t[2] for t in top[o:o + SUBLANES]], sub)
        e1 = jnp.floor(code * (1.0 / PEER_KEYS))
        out.append((e / tot, e1, code - e1 * float(PEER_KEYS)))
    return out


def _peer_head(h, qp_ref, k1_ref, k2_ref, e1t_scr, e2t_scr, gt_scr):
    s1 = _dot_nt(k1_ref[h], qp_ref[2 * h])
    s2 = _dot_nt(k2_ref[h], qp_ref[2 * h + 1])
    base = pl.multiple_of(h * PEER_TOPK, PEER_TOPK)
    for o, (gate, e1, e2) in zip((0, SUBLANES), _head_experts(s1, s2)):
        rows = pl.ds(pl.multiple_of(base + o, SUBLANES), SUBLANES)
        gt_scr[rows, :] = gate
        e1t_scr[rows, :] = e1
        e2t_scr[rows, :] = e2


def _topk_body(qp_ref, k1_ref, k2_ref, e1_ref, e2_ref, g_ref, e1t_scr, e2t_scr, gt_scr):
    def head_group(i, carry):
        for o in range(TOPK_HEADS_PER_ITER):
            _peer_head(TOPK_HEADS_PER_ITER * i + o, qp_ref, k1_ref, k2_ref, e1t_scr, e2t_scr, gt_scr)
        return carry

    lax.fori_loop(0, PEER_HEADS // TOPK_HEADS_PER_ITER, head_group, 0)
    e1_ref[...] = e1t_scr[...].T.astype(I32)
    e2_ref[...] = e2t_scr[...].T.astype(I32)
    g_ref[...] = gt_scr[...].T


def _topk(qp, k1, k2):
    nq, tokens, _ = qp.shape
    tm = TM_TOPK
    assert tm == PEER_SLOTS
    row = lambda i: (i, 0)
    const3 = lambda i: (0, 0, 0)
    return pl.pallas_call(
        _topk_body,
        out_shape=(
            jax.ShapeDtypeStruct((tokens, PEER_SLOTS), I32),
            jax.ShapeDtypeStruct((tokens, PEER_SLOTS), I32),
            jax.ShapeDtypeStruct((tokens, PEER_SLOTS), F32),
        ),
        grid=(tokens // tm,),
        in_specs=[
            pl.BlockSpec((nq, tm, PEER_HALF), lambda i: (0, i, 0)),
            pl.BlockSpec((PEER_HEADS, PEER_KEYS, PEER_HALF), const3),
            pl.BlockSpec((PEER_HEADS, PEER_KEYS, PEER_HALF), const3),
        ],
        out_specs=(
            pl.BlockSpec((tm, PEER_SLOTS), row),
            pl.BlockSpec((tm, PEER_SLOTS), row),
            pl.BlockSpec((tm, PEER_SLOTS), row),
        ),
        scratch_shapes=[
            pltpu.VMEM((PEER_SLOTS, tm), F32),
            pltpu.VMEM((PEER_SLOTS, tm), F32),
            pltpu.VMEM((PEER_SLOTS, tm), F32),
        ],
        compiler_params=pltpu.CompilerParams(
            dimension_semantics=("arbitrary",), vmem_limit_bytes=VMEM_LIMIT),
        name="topk",
    )(qp, k1, k2)


def _peer_u_body(h2_ref, ut_ref, e1_ref, e2_ref, g_ref, w_ref, acc_ref):
    j = pl.program_id(1)

    @pl.when(j == 0)
    def _():
        acc_ref[...] = jnp.zeros_like(acc_ref)

    scores = _dot(h2_ref[...], ut_ref[...])
    e1 = e1_ref[...]
    e2 = e2_ref[...]
    acc = acc_ref[...]
    for c in range(EC_U // PEER_KEYS):
        picked = jnp.take_along_axis(scores[:, c * PEER_KEYS:(c + 1) * PEER_KEYS], e2, axis=1)
        acc = jnp.where(e1 == j * (EC_U // PEER_KEYS) + c, picked, acc)
    acc_ref[...] = acc

    @pl.when(j == pl.num_programs(1) - 1)
    def _():
        w_ref[...] = g_ref[...] * (0.5 * acc * (1.0 + lax.erf(acc * np.float32(np.sqrt(0.5)))))


def _peer_u(h2, u_t, e1, e2, gate):
    tokens = h2.shape[0]
    tm = TM_U
    row = lambda i, j: (i, 0)
    return pl.pallas_call(
        _peer_u_body,
        out_shape=jax.ShapeDtypeStruct((tokens, PEER_SLOTS), F32),
        grid=(tokens // tm, PEER_EXPERTS // EC_U),
        in_specs=[
            pl.BlockSpec((tm, D_MODEL), row),
            pl.BlockSpec((D_MODEL, EC_U), lambda i, j: (0, j)),
            pl.BlockSpec((tm, PEER_SLOTS), row),
            pl.BlockSpec((tm, PEER_SLOTS), row),
            pl.BlockSpec((tm, PEER_SLOTS), row),
        ],
        out_specs=pl.BlockSpec((tm, PEER_SLOTS), row),
        scratch_shapes=[pltpu.VMEM((tm, PEER_SLOTS), F32)],
        compiler_params=pltpu.CompilerParams(
            dimension_semantics=("arbitrary", "arbitrary"), vmem_limit_bytes=VMEM_LIMIT),
        name="peer_u",
    )(h2, u_t, e1, e2, gate)


def _peer_v_body(e1_ref, e2_ref, w_ref, v_ref, x1_ref, y_ref, wd_scr, acc_scr):
    tm = e1_ref.shape[0]
    j = pl.program_id(1)

    @pl.when(j == 0)
    def _():
        eiota = lax.broadcasted_iota(I32, (PEER_KEYS, PEER_SLOTS), 0)

        def token(t, carry):
            e1 = e1_ref[pl.ds(t, 1), :]
            e2 = e2_ref[pl.ds(t, 1), :]
            wt = w_ref[pl.ds(t, 1), :]
            a = jnp.where(eiota == e1, wt, 0.0).astype(BF16)
            b = jnp.where(eiota == e2, 1.0, 0.0).astype(BF16)
            wd_scr[pl.ds(t, PEER_KEYS, stride=tm), :] = _dot_nt(a, b)
            return carry

        lax.fori_loop(0, tm, token, 0, unroll=8)
        acc_scr[...] = jnp.zeros_like(acc_scr)

    n_e1 = EC_V // PEER_KEYS
    parts = [wd_scr[pl.ds(pl.multiple_of((j * n_e1 + c) * tm, tm), tm), :] for c in range(n_e1)]
    dense_w = jnp.concatenate(parts, axis=-1).astype(BF16)
    acc_scr[...] += _dot(dense_w, v_ref[...])

    @pl.when(j == pl.num_programs(1) - 1)
    def _():
        y_ref[...] = x1_ref[...] + acc_scr[...]


def _peer_v(e1, e2, w, v_bf16, x1):
    tokens = x1.shape[0]
    tm = TM_V
    row = lambda i, j: (i, 0)
    return pl.pallas_call(
        _peer_v_body,
        out_shape=jax.ShapeDtypeStruct((tokens, D_MODEL), F32),
        grid=(tokens // tm, PEER_EXPERTS // EC_V),
        in_specs=[
            pl.BlockSpec((tm, PEER_SLOTS), row),
            pl.BlockSpec((tm, PEER_SLOTS), row),
            pl.BlockSpec((tm, PEER_SLOTS), row),
            pl.BlockSpec((EC_V, D_MODEL), lambda i, j: (j, 0)),
            pl.BlockSpec((tm, D_MODEL), row),
        ],
        out_specs=pl.BlockSpec((tm, D_MODEL), row),
        scratch_shapes=[
            pltpu.VMEM((PEER_KEYS * tm, PEER_KEYS), F32),
            pltpu.VMEM((tm, D_MODEL), F32),
        ],
        compiler_params=pltpu.CompilerParams(
            dimension_semantics=("arbitrary", "arbitrary"), vmem_limit_bytes=VMEM_LIMIT),
        name="peer_v",
    )(e1, e2, w, v_bf16, x1)


def _rope_tables(seq):
    inv = ROPE_THETA ** (-jnp.arange(0, ROPE_DIM, 2, dtype=F32) / ROPE_DIM)
    ang = jnp.arange(seq, dtype=F32)[:, None] * inv[None, :]
    cos = jnp.cos(ang)
    sin = jnp.sin(ang)
    lane = np.arange(LANES)
    in_head = lane % HEAD_DIM
    freq = in_head % ROPE_HALF
    cos_l = jnp.where(in_head < ROPE_DIM, cos[:, freq], 1.0)
    sin_l = jnp.where(in_head < ROPE_HALF, -sin[:, freq], jnp.where(in_head < ROPE_DIM, sin[:, freq], 0.0))
    return cos_l.astype(F32), sin_l.astype(F32)


def _layer(x, p):
    batch, seq, _ = x.shape
    tokens = batch * seq
    x2d = x.reshape(tokens, D_MODEL)
    cos_t, sin_t = _rope_tables(seq)
    qa, ka, va, qb, kb, vb, gates = _inproj(
        x2d, seq, p["norm1"], p["w_in"], p["gqa"], p["gka"], p["gqb"], p["gkb"], cos_t, sin_t, p["bd"])

    oa, = _banded_attention(
        qa.reshape(batch, seq, A_Q_W), ka.reshape(batch, seq, A_KV_W), va.reshape(batch, seq, A_KV_W),
        window=A_WINDOW, dil=1, q_block=lambda r: 0, kv_block=lambda r: 0, nkv=A_KV_HEADS, grp=A_GROUP,
        sink=p["sink"], name="attn_a")

    o_groups, lse_groups = [], []
    for gi, (win, dil) in enumerate(B_PATTERNS):
        sub = seq // dil
        blk = functools.partial(lambda r, gi: r * B_GROUPS + gi, gi=gi)
        o, lse = _banded_attention(
            qb.reshape(batch, sub, dil * B_W), kb.reshape(batch, sub, dil * B_W), vb.reshape(batch, sub, dil * B_W),
            window=win // (2 * dil), dil=dil, q_block=blk, kv_block=blk, nkv=B_HEADS_PER_GROUP, grp=1,
            want_lse=True, out_dtype=F32, name=f"attn_b{gi}")
        o_groups.append(o.reshape(tokens, B_GROUP_W))
        lse_groups.append(lse.reshape(tokens, B_GROUP_W))

    x1, h2, qp = _post(oa.reshape(tokens, A_Q_W), o_groups, lse_groups, gates, x2d,
                       p["w_o_a"], p["w_o_b"], p["w_out"], p["norm2"], p["w_query"])
    e1, e2, gate = _topk(qp, p["k1"], p["k2"])
    w = _peer_u(h2, p["u_t"], e1, e2, gate)
    y = _peer_v(e1, e2, w, p["v"], x1)
    return y.reshape(batch, seq, D_MODEL)


def _tile_gain(g):
    return jnp.tile(g.astype(F32), LANES // HEAD_DIM).reshape(1, LANES)


def kernel(x_prompt, x_sample, norm1, w_in, q_norm_a, k_norm_a, sink_a, q_norm_b, k_norm_b, w_o_a, w_o_b, w_out,
           norm2, w_query, sub_keys_1, sub_keys_2, expert_u, expert_v):
    depth = norm1.shape[0]
    head_of_lane = np.arange(LANES) // HEAD_DIM
    bd = jnp.asarray((head_of_lane[:, None] == head_of_lane[None, :]) / HEAD_DIM, dtype=BF16)
    params = []
    for l in range(depth):
        params.append({
            "norm1": norm1[l].reshape(1, D_MODEL),
            "w_in": w_in[l].astype(BF16),
            "gqa": _tile_gain(q_norm_a[l]), "gka": _tile_gain(k_norm_a[l]),
            "gqb": _tile_gain(q_norm_b[l]), "gkb": _tile_gain(k_norm_b[l]),
            "sink": sink_a[l].astype(F32),
            "w_o_a": w_o_a[l].astype(BF16), "w_o_b": w_o_b[l].astype(BF16), "w_out": w_out[l].astype(BF16),
            "norm2": norm2[l].reshape(1, D_MODEL),
            "w_query": w_query[l].astype(BF16),
            "k1": sub_keys_1[l].astype(BF16), "k2": sub_keys_2[l].astype(BF16),
            "u_t": expert_u[l].astype(BF16).T,
            "v": expert_v[l].astype(BF16),
            "bd": bd,
        })
    ys = []
    for x in (x_prompt, x_sample):
        for p in params:
            x = _layer(x, p)
        ys.append(x)
    return tuple(ys)
```

```python
import functools

import numpy as np
import jax
import jax.numpy as jnp
from jax import lax
from jax.experimental import pallas as pl
from jax.experimental.pallas import tpu as pltpu

F32 = jnp.float32
BF16 = jnp.bfloat16
I32 = jnp.int32

D_MODEL = 1024
HEAD_DIM = 64
ROPE_DIM = HEAD_DIM // 4
ROPE_HALF = ROPE_DIM // 2
ROPE_THETA = 500000.0
EPS = 1e-6
A_Q_HEADS = 8
A_KV_HEADS = 2
A_GROUP = A_Q_HEADS // A_KV_HEADS
A_WINDOW = 128
B_PATTERNS = ((128, 1), (512, 4), (2048, 16))
B_HEADS_PER_GROUP = 4
B_GROUPS = len(B_PATTERNS)
A_Q_W = A_Q_HEADS * HEAD_DIM
A_KV_W = A_KV_HEADS * HEAD_DIM
B_GROUP_W = B_HEADS_PER_GROUP * HEAD_DIM
B_W = B_GROUPS * B_GROUP_W
IN_COLS = A_Q_W + 2 * A_KV_W + 3 * B_W + 2 * D_MODEL
PEER_HEADS = 8
PEER_KEYS = 128
PEER_EXPERTS = PEER_KEYS * PEER_KEYS
PEER_HALF = 128
PEER_TOPK = 16
PEER_SLOTS = PEER_HEADS * PEER_TOPK
PEER_Q_W = PEER_HEADS * 2 * PEER_HALF

LANES = 128
SUBLANES = 8
MASK_VALUE = -1e30
NEG = float(np.finfo(np.float32).min)
VMEM_LIMIT = 56 * 1024 * 1024

OFF_QA = 0
OFF_KA = OFF_QA + A_Q_W
OFF_VA = OFF_KA + A_KV_W
OFF_QB = OFF_VA + A_KV_W
OFF_KB = OFF_QB + B_W
OFF_VB = OFF_KB + B_W
OFF_GA = OFF_VB + B_W
OFF_GB = OFF_GA + D_MODEL

TM_IN = 256
PREP_COLS = 256
TM_POST = 512
TM_TOPK = 128
TOPK_HEADS_PER_ITER = 4
TM_U = 2048
EC_U = 512
GATHER_ROWS = 8
TM_V = 512
EC_V = 1024
ATTN_TQ = 256

def _dot(a, b):
    return jnp.dot(a, b, preferred_element_type=F32)


def _dot_nt(a, b):
    return lax.dot_general(a, b, (((1,), (1,)), ((), ())), preferred_element_type=F32)


def _rms(x, g):
    ms = jnp.mean(x * x, axis=-1, keepdims=True)
    return (x * lax.rsqrt(ms + EPS)) * g


def _head_prep(t, gain, cosv, sinv, bd, lane_lo):
    sq = t * t
    hi = sq.astype(BF16)
    lo = (sq - hi.astype(F32)).astype(BF16)
    ms = _dot(hi, bd) + _dot(lo, bd)
    tn = (t * lax.rsqrt(ms + EPS)) * gain
    out = []
    for c in range(t.shape[1] // LANES):
        tc = tn[:, c * LANES:(c + 1) * LANES]
        partner = jnp.where(lane_lo, pltpu.roll(tc, LANES - ROPE_HALF, 1), pltpu.roll(tc, ROPE_HALF, 1))
        out.append(tc * cosv + partner * sinv)
    return out


def _inproj_body(x_ref, n1_ref, w_ref, gqa_ref, gka_ref, gqb_ref, gkb_ref, cos_ref, sin_ref, bd_ref,
                 qa_ref, ka_ref, va_ref, qb_ref, kb_ref, vb_ref, gate_ref):
    h = _rms(x_ref[...], n1_ref[...]).astype(BF16)
    cosv = cos_ref[...]
    sinv = sin_ref[...]
    bd = bd_ref[...]
    lane = lax.broadcasted_iota(I32, (1, LANES), 1)
    lane_lo = (lane % HEAD_DIM) < ROPE_HALF

    def prepped(off, width, gain_ref, scale, out_ref):
        step = min(width, PREP_COLS)
        gain = jnp.concatenate([gain_ref[...]] * (step // LANES), axis=-1)
        for c in range(width // step):
            t = _dot(h, w_ref[:, off + c * step: off + (c + 1) * step])
            for k, tk in enumerate(_head_prep(t, gain, cosv, sinv, bd[:step, :step], lane_lo)):
                if scale != 1.0:
                    tk = tk * scale
                lo = c * step + k * LANES
                out_ref[:, lo:lo + LANES] = tk.astype(out_ref.dtype)

    qscale = HEAD_DIM ** -0.5
    prepped(OFF_QA, A_Q_W, gqa_ref, qscale, qa_ref)
    prepped(OFF_KA, A_KV_W, gka_ref, 1.0, ka_ref)
    va_ref[...] = _dot(h, w_ref[:, OFF_VA:OFF_VA + A_KV_W]).astype(va_ref.dtype)
    prepped(OFF_QB, B_W, gqb_ref, qscale, qb_ref)
    prepped(OFF_KB, B_W, gkb_ref, 1.0, kb_ref)
    vb_ref[...] = _dot(h, w_ref[:, OFF_VB:OFF_VB + B_W]).astype(vb_ref.dtype)
    for c in range(2 * D_MODEL // 512):
        g = _dot(h, w_ref[:, OFF_GA + c * 512: OFF_GA + (c + 1) * 512])
        gate_ref[:, c * 512:(c + 1) * 512] = jax.nn.sigmoid(g)


def _inproj(x2d, seq, n1, w_in, gqa, gka, gqb, gkb, cos_t, sin_t, bd):
    tokens = x2d.shape[0]
    tm = TM_IN
    blocks_per_seq = seq // tm
    row = lambda i: (i, 0)
    const = lambda i: (0, 0)
    pos = lambda i: (i % blocks_per_seq, 0)
    out_shape = (
        jax.ShapeDtypeStruct((tokens, A_Q_W), BF16),
        jax.ShapeDtypeStruct((tokens, A_KV_W), BF16),
        jax.ShapeDtypeStruct((tokens, A_KV_W), BF16),
        jax.ShapeDtypeStruct((tokens, B_W), BF16),
        jax.ShapeDtypeStruct((tokens, B_W), BF16),
        jax.ShapeDtypeStruct((tokens, B_W), BF16),
        jax.ShapeDtypeStruct((tokens, 2 * D_MODEL), F32),
    )
    return pl.pallas_call(
        _inproj_body,
        out_shape=out_shape,
        grid=(tokens // tm,),
        in_specs=[
            pl.BlockSpec((tm, D_MODEL), row),
            pl.BlockSpec((1, D_MODEL), const),
            pl.BlockSpec((D_MODEL, IN_COLS), const),
            pl.BlockSpec((1, LANES), const),
            pl.BlockSpec((1, LANES), const),
            pl.BlockSpec((1, LANES), const),
            pl.BlockSpec((1, LANES), const),
            pl.BlockSpec((tm, LANES), pos),
            pl.BlockSpec((tm, LANES), pos),
            pl.BlockSpec((PREP_COLS, PREP_COLS), const),
        ],
        out_specs=(
            pl.BlockSpec((tm, A_Q_W), row),
            pl.BlockSpec((tm, A_KV_W), row),
            pl.BlockSpec((tm, A_KV_W), row),
            pl.BlockSpec((tm, B_W), row),
            pl.BlockSpec((tm, B_W), row),
            pl.BlockSpec((tm, B_W), row),
            pl.BlockSpec((tm, 2 * D_MODEL), row),
        ),
        compiler_params=pltpu.CompilerParams(
            dimension_semantics=("arbitrary",), vmem_limit_bytes=VMEM_LIMIT),
        name="inproj",
    )(x2d, n1, w_in, gqa, gka, gqb, gkb, cos_t, sin_t, bd)


def _attn_body(*refs, window, tq, kw, length, nkv, grp, has_sink, want_lse):
    refs = list(refs)
    sink_ref = refs.pop(0) if has_sink else None
    q_ref, k_ref, v_ref, o_ref = refs[:4]
    lse_ref = refs[4] if want_lse else None

    q0 = pl.program_id(2) * tq
    if kw == length:
        start = 0
        kwin = k_ref[...]
        vwin = v_ref[...]
    else:
        start = pl.multiple_of(jnp.clip(q0 - window, 0, length - kw), window)
        kwin = k_ref[pl.ds(start, kw), :]
        vwin = v_ref[pl.ds(start, kw), :]
    q = q_ref[...]

    rows = grp * tq
    qi = lax.broadcasted_iota(I32, (rows, kw), 0) & (tq - 1)
    kj = lax.broadcasted_iota(I32, (rows, kw), 1)
    valid = jnp.abs(kj - qi + (start - q0)) <= window

    o_parts = [None] * (nkv * grp)
    lse_parts = [None] * (nkv * grp)
    for j in range(nkv):
        kh = kwin[:, j * HEAD_DIM:(j + 1) * HEAD_DIM]
        vh = vwin[:, j * HEAD_DIM:(j + 1) * HEAD_DIM]
        heads = [j * grp + g for g in range(grp)]
        qh = [q[:, h * HEAD_DIM:(h + 1) * HEAD_DIM] for h in heads]
        qh = qh[0] if grp == 1 else jnp.concatenate(qh, axis=0)
        s = jnp.where(valid, _dot_nt(qh, kh), MASK_VALUE)
        m = jnp.max(s, axis=-1, keepdims=True)
        if has_sink:
            sk = [jnp.full((tq, 1), sink_ref[h], F32) for h in heads]
            sk = sk[0] if grp == 1 else jnp.concatenate(sk, axis=0)
            m = jnp.maximum(m, sk)
        p = jnp.exp(s - m)
        denom = jnp.sum(p, axis=-1, keepdims=True)
        if has_sink:
            denom = denom + jnp.exp(sk - m)
        o = _dot(p.astype(BF16), vh) / denom
        lse = m + jnp.log(denom) if want_lse else None
        for g, h in enumerate(heads):
            o_parts[h] = o[g * tq:(g + 1) * tq]
            if want_lse:
                lse_parts[h] = jnp.broadcast_to(lse[g * tq:(g + 1) * tq], (tq, HEAD_DIM))
    o_ref[...] = jnp.concatenate(o_parts, axis=-1).astype(o_ref.dtype)
    if want_lse:
        lse_ref[...] = jnp.concatenate(lse_parts, axis=-1)


def _banded_attention(q, k, v, *, window, dil, q_block, kv_block, nkv, grp, sink=None, want_lse=False,
                      out_dtype=BF16, name):
    batch, length, _ = q.shape
    wq = nkv * grp * HEAD_DIM
    wk = nkv * HEAD_DIM
    tq = min(ATTN_TQ, length)
    kw = min(length, tq + 2 * window)
    out_cols = dil * wq
    body = functools.partial(_attn_body, window=window, tq=tq, kw=kw, length=length, nkv=nkv, grp=grp,
                             has_sink=sink is not None, want_lse=want_lse)
    in_specs = [
        pl.BlockSpec((None, tq, wq), lambda b, r, t: (b, t, q_block(r))),
        pl.BlockSpec((None, length, wk), lambda b, r, t: (b, 0, kv_block(r))),
        pl.BlockSpec((None, length, wk), lambda b, r, t: (b, 0, kv_block(r))),
    ]
    args = [q, k, v]
    if sink is not None:
        in_specs.insert(0, pl.BlockSpec(memory_space=pltpu.SMEM))
        args.insert(0, sink)
    o_spec = pl.BlockSpec((None, tq, wq), lambda b, r, t: (b, t, r))
    out_shape = [jax.ShapeDtypeStruct((batch, length, out_cols), out_dtype)]
    out_specs = [o_spec]
    if want_lse:
        out_shape.append(jax.ShapeDtypeStruct((batch, length, out_cols), F32))
        out_specs.append(o_spec)
    return pl.pallas_call(
        body,
        out_shape=tuple(out_shape),
        grid=(batch, dil, length // tq),
        in_specs=in_specs,
        out_specs=tuple(out_specs),
        compiler_params=pltpu.CompilerParams(
            dimension_semantics=("arbitrary", "arbitrary", "arbitrary"), vmem_limit_bytes=VMEM_LIMIT),
        name=name,
    )(*args)


def _post_body(oa_ref, o1_ref, o2_ref, o3_ref, l1_ref, l2_ref, l3_ref, ga_ref, gb_ref, x_ref,
               woa_ref, wob_ref, wout_ref, n2_ref, wq_ref,
               x1_ref, h2_ref, qp_ref):
    l1, l2, l3 = l1_ref[...], l2_ref[...], l3_ref[...]
    mx = jnp.maximum(jnp.maximum(l1, l2), l3)
    e1, e2, e3 = jnp.exp(l1 - mx), jnp.exp(l2 - mx), jnp.exp(l3 - mx)
    tot = e1 + e2 + e3
    ob = (e1 / tot) * o1_ref[...] + (e2 / tot) * o2_ref[...] + (e3 / tot) * o3_ref[...]
    ya = _dot(oa_ref[...], woa_ref[...])
    yb = _dot(ob.astype(BF16), wob_ref[...])
    merged = ga_ref[...] * ya + gb_ref[...] * yb
    x1 = x_ref[...] + _dot(merged.astype(BF16), wout_ref[...])
    x1_ref[...] = x1
    h2 = _rms(x1, n2_ref[...]).astype(BF16)
    h2_ref[...] = h2
    for c in range(PEER_Q_W // PEER_HALF):
        qp_ref[c] = _dot(h2, wq_ref[:, c * PEER_HALF:(c + 1) * PEER_HALF]).astype(qp_ref.dtype)


def _post(oa, o_groups, lse_groups, gates, x2d, w_o_a, w_o_b, w_out, n2, w_query):
    tokens = x2d.shape[0]
    tm = TM_POST
    row = lambda i: (i, 0)
    const = lambda i: (0, 0)
    nq = PEER_Q_W // PEER_HALF
    return pl.pallas_call(
        _post_body,
        out_shape=(
            jax.ShapeDtypeStruct((tokens, D_MODEL), F32),
            jax.ShapeDtypeStruct((tokens, D_MODEL), BF16),
            jax.ShapeDtypeStruct((nq, tokens, PEER_HALF), BF16),
        ),
        grid=(tokens // tm,),
        in_specs=[pl.BlockSpec((tm, A_Q_W), row)]
        + [pl.BlockSpec((tm, B_GROUP_W), row)] * (2 * B_GROUPS)
        + [
            pl.BlockSpec((tm, D_MODEL), lambda i: (i, 0)),
            pl.BlockSpec((tm, D_MODEL), lambda i: (i, 1)),
            pl.BlockSpec((tm, D_MODEL), row),
            pl.BlockSpec((A_Q_W, D_MODEL), const),
            pl.BlockSpec((B_GROUP_W, D_MODEL), const),
            pl.BlockSpec((D_MODEL, D_MODEL), const),
            pl.BlockSpec((1, D_MODEL), const),
            pl.BlockSpec((D_MODEL, PEER_Q_W), const),
        ],
        out_specs=(
            pl.BlockSpec((tm, D_MODEL), row),
            pl.BlockSpec((tm, D_MODEL), row),
            pl.BlockSpec((nq, tm, PEER_HALF), lambda i: (0, i, 0)),
        ),
        compiler_params=pltpu.CompilerParams(
            dimension_semantics=("arbitrary",), vmem_limit_bytes=VMEM_LIMIT),
        name="post",
    )(oa, *o_groups, *lse_groups, gates, gates, x2d, w_o_a, w_o_b, w_out, n2, w_query)


def _tree(xs, op):
    xs = list(xs)
    while len(xs) > 1:
        nxt = [op(xs[i], xs[i + 1]) for i in range(0, len(xs) - 1, 2)]
        if len(xs) % 2:
            nxt.append(xs[-1])
        xs = nxt
    return xs[0]


def _all_rows(blocks, op):
    r = _tree(blocks, op)
    for shift in (4, 2, 1):
        r = op(r, pltpu.roll(r, shift, 0))
    return r


def _pack(rows, sub):
    blk = rows[0]
    for i in range(1, SUBLANES):
        blk = jnp.where(sub == i, rows[i], blk)
    return blk


def _select_top(blocks, ids, payload, k):
    past = float(np.finfo(np.float32).max)
    out = []
    for _ in range(k):
        m = _all_rows(blocks, jnp.maximum)
        pos = _all_rows([jnp.where(b == m, i, past) for b, i in zip(blocks, ids)], jnp.minimum)
        sel = [i == pos for i in ids]
        picked = None
        if payload is not None:
            picked = _all_rows([jnp.where(s, p, -1.0) for s, p in zip(sel, payload)], jnp.maximum)
        blocks = [jnp.where(s, NEG, b) for s, b in zip(sel, blocks)]
        out.append((m, pos, picked))
    return out


def _candidate_rows(first, second, combine, sub):
    half = SUBLANES
    first_hi = pltpu.roll(_pack(first[half:], sub), 4, 0)
    lo = _pack(second[:half], sub)
    hi = _pack(second[half:], sub)
    w = jnp.where
    a_blocks = [first[0], first[0], first[1],
                w(sub < 5, first[2], first[5]),
                w(sub < 4, first[3], first[4]),
                w(sub < 2, first[6], w(sub < 4, first[7], first_hi)),
                first_hi]
    b_blocks = [lo, hi, lo,
                w(sub < 5, lo, pltpu.roll(lo, 5, 0)),
                w(sub < 4, lo, pltpu.roll(lo, 4, 0)),
                w(sub < 2, lo, w(sub < 4, pltpu.roll(lo, 2, 0), second[0])),
                second[0]]
    t = PEER_TOPK
    flat = [sub, half + sub, t + sub,
            w(sub < 5, 2 * t + sub, 5 * t + sub - 5),
            w(sub < 4, 3 * t + sub, 4 * t + sub - 4),
            w(sub < 2, 6 * t + sub, w(sub < 4, 7 * t + sub - 2, (half + sub - 4) * t)),
            (half + 4 + sub) * t]
    used = [None, None, None, sub < 7, sub < 7, None, sub < 4]
    flat = [f if u is None else w(u, f, -1) for f, u in zip(flat, used)]
    return [combine(a, b) for a, b in zip(a_blocks, b_blocks)], flat


def _head_experts(s1, s2):
    tm = s1.shape[1]
    sub = lax.broadcasted_iota(I32, (SUBLANES, tm), 0)
    subf = sub.astype(F32)
    key_ids = [subf + float(SUBLANES * g) for g in range(PEER_KEYS // SUBLANES)]

    def half(s):
        top = _select_top([s[SUBLANES * g:SUBLANES * (g + 1)] for g in range(PEER_KEYS // SUBLANES)], key_ids,
                          None, PEER_TOPK)
        return [t[0] for t in top], [t[1] for t in top]

    v1, i1 = half(s1)
    v2, i2 = half(s2)
    cand, flat = _candidate_rows(v1, v2, lambda a, b: a + b, sub)
    cand = [jnp.where(f >= 0, c, NEG) for c, f in zip(cand, flat)]
    expert, _ = _candidate_rows(i1, i2, lambda a, b: a * float(PEER_KEYS) + b, sub)
    unused_base = PEER_TOPK * PEER_TOPK
    ids = [jnp.where(f >= 0, f, unused_base + SUBLANES * g + sub).astype(F32) for g, f in enumerate(flat)]
    top = _select_top(cand, ids, expert, PEER_TOPK)
    ex = [jnp.exp(_pack([t[0] for t in top[o:o + SUBLANES]], sub) - top[0][0]) for o in (0, SUBLANES)]
    tot = _all_rows(ex, jnp.add)
    out = []
    for o, e in zip((0, SUBLANES), ex):
        code = _pack([t[2] for t in top[o:o + SUBLANES]], sub)
        e1 = jnp.floor(code * (1.0 / PEER_KEYS))
        out.append((e / tot, e1, code - e1 * float(PEER_KEYS)))
    return out


def _peer_head(h, qp_ref, k1_ref, k2_ref, e1t_scr, e2t_scr, gt_scr):
    s1 = _dot_nt(k1_ref[h], qp_ref[2 * h])
    s2 = _dot_nt(k2_ref[h], qp_ref[2 * h + 1])
    base = pl.multiple_of(h * PEER_TOPK, PEER_TOPK)
    for o, (gate, e1, e2) in zip((0, SUBLANES), _head_experts(s1, s2)):
        rows = pl.ds(pl.multiple_of(base + o, SUBLANES), SUBLANES)
        gt_scr[rows, :] = gate
        e1t_scr[rows, :] = e1
        e2t_scr[rows, :] = e2


def _topk_body(qp_ref, k1_ref, k2_ref, e1_ref, e2_ref, g_ref, e1t_scr, e2t_scr, gt_scr):
    def head_group(i, carry):
        for o in range(TOPK_HEADS_PER_ITER):
            _peer_head(TOPK_HEADS_PER_ITER * i + o, qp_ref, k1_ref, k2_ref, e1t_scr, e2t_scr, gt_scr)
        return carry

    lax.fori_loop(0, PEER_HEADS // TOPK_HEADS_PER_ITER, head_group, 0)
    e1_ref[...] = e1t_scr[...].T.astype(I32)
    e2_ref[...] = e2t_scr[...].T.astype(I32)
    g_ref[...] = gt_scr[...].T


def _topk(qp, k1, k2):
    nq, tokens, _ = qp.shape
    tm = TM_TOPK
    assert tm == PEER_SLOTS
    row = lambda i: (i, 0)
    const3 = lambda i: (0, 0, 0)
    return pl.pallas_call(
        _topk_body,
        out_shape=(
            jax.ShapeDtypeStruct((tokens, PEER_SLOTS), I32),
            jax.ShapeDtypeStruct((tokens, PEER_SLOTS), I32),
            jax.ShapeDtypeStruct((tokens, PEER_SLOTS), F32),
        ),
        grid=(tokens // tm,),
        in_specs=[
            pl.BlockSpec((nq, tm, PEER_HALF), lambda i: (0, i, 0)),
            pl.BlockSpec((PEER_HEADS, PEER_KEYS, PEER_HALF), const3),
            pl.BlockSpec((PEER_HEADS, PEER_KEYS, PEER_HALF), const3),
        ],
        out_specs=(
            pl.BlockSpec((tm, PEER_SLOTS), row),
            pl.BlockSpec((tm, PEER_SLOTS), row),
            pl.BlockSpec((tm, PEER_SLOTS), row),
        ),
        scratch_shapes=[
            pltpu.VMEM((PEER_SLOTS, tm), F32),
            pltpu.VMEM((PEER_SLOTS, tm), F32),
            pltpu.VMEM((PEER_SLOTS, tm), F32),
        ],
        compiler_params=pltpu.CompilerParams(
            dimension_semantics=("arbitrary",), vmem_limit_bytes=VMEM_LIMIT),
        name="topk",
    )(qp, k1, k2)


def _peer_u_body(h2_ref, ut_ref, e1_ref, e2_ref, g_ref, w_ref, acc_ref, sc_ref):
    j = pl.program_id(1)
    n_e1 = EC_U // PEER_KEYS

    @pl.when(j == 0)
    def _():
        acc_ref[...] = jnp.zeros_like(acc_ref)
        sc_ref[...] = jnp.zeros_like(sc_ref)

    prev = sc_ref.at[(j + 1) % 2]
    for r in range(0, h2_ref.shape[0], GATHER_ROWS):
        rows = slice(r, r + GATHER_ROWS)
        e1 = e1_ref[rows, :]
        e2 = e2_ref[rows, :]
        acc = acc_ref[rows, :]
        for c in range(n_e1):
            picked = jnp.take_along_axis(prev[rows, c * PEER_KEYS:(c + 1) * PEER_KEYS], e2, axis=1)
            acc = jnp.where(e1 == (j - 1) * n_e1 + c, picked, acc)
        acc_ref[rows, :] = acc
    sc_ref[j % 2] = _dot(h2_ref[...], ut_ref[...])

    @pl.when(j == pl.num_programs(1) - 1)
    def _():
        acc = acc_ref[...]
        w_ref[...] = g_ref[...] * (0.5 * acc * (1.0 + lax.erf(acc * np.float32(np.sqrt(0.5)))))


def _peer_u(h2, u_t, e1, e2, gate):
    tokens = h2.shape[0]
    tm = TM_U
    n_chunks = PEER_EXPERTS // EC_U
    row = lambda i, j: (i, 0)
    return pl.pallas_call(
        _peer_u_body,
        out_shape=jax.ShapeDtypeStruct((tokens, PEER_SLOTS), F32),
        grid=(tokens // tm, n_chunks + 1),
        in_specs=[
            pl.BlockSpec((tm, D_MODEL), row),
            pl.BlockSpec((D_MODEL, EC_U), lambda i, j: (0, jnp.minimum(j, n_chunks - 1))),
            pl.BlockSpec((tm, PEER_SLOTS), row),
            pl.BlockSpec((tm, PEER_SLOTS), row),
            pl.BlockSpec((tm, PEER_SLOTS), row),
        ],
        out_specs=pl.BlockSpec((tm, PEER_SLOTS), row),
        scratch_shapes=[pltpu.VMEM((tm, PEER_SLOTS), F32), pltpu.VMEM((2, tm, EC_U), F32)],
        compiler_params=pltpu.CompilerParams(
            dimension_semantics=("arbitrary", "arbitrary"), vmem_limit_bytes=VMEM_LIMIT),
        name="peer_u",
    )(h2, u_t, e1, e2, gate)


def _peer_v_body(e1_ref, e2_ref, w_ref, v_ref, x1_ref, y_ref, wd_scr, acc_scr):
    tm = e1_ref.shape[0]
    j = pl.program_id(1)

    @pl.when(j == 0)
    def _():
        eiota = lax.broadcasted_iota(I32, (PEER_KEYS, PEER_SLOTS), 0)

        def token(t, carry):
            e1 = e1_ref[pl.ds(t, 1), :]
            e2 = e2_ref[pl.ds(t, 1), :]
            wt = w_ref[pl.ds(t, 1), :]
            a = jnp.where(eiota == e1, wt, 0.0).astype(BF16)
            b = jnp.where(eiota == e2, 1.0, 0.0).astype(BF16)
            wd_scr[pl.ds(t, PEER_KEYS, stride=tm), :] = _dot_nt(a, b)
            return carry

        lax.fori_loop(0, tm, token, 0, unroll=8)
        acc_scr[...] = jnp.zeros_like(acc_scr)

    n_e1 = EC_V // PEER_KEYS
    parts = [wd_scr[pl.ds(pl.multiple_of((j * n_e1 + c) * tm, tm), tm), :] for c in range(n_e1)]
    dense_w = jnp.concatenate(parts, axis=-1).astype(BF16)
    acc_scr[...] += _dot(dense_w, v_ref[...])

    @pl.when(j == pl.num_programs(1) - 1)
    def _():
        y_ref[...] = x1_ref[...] + acc_scr[...]


def _peer_v(e1, e2, w, v_bf16, x1):
    tokens = x1.shape[0]
    tm = TM_V
    row = lambda i, j: (i, 0)
    return pl.pallas_call(
        _peer_v_body,
        out_shape=jax.ShapeDtypeStruct((tokens, D_MODEL), F32),
        grid=(tokens // tm, PEER_EXPERTS // EC_V),
        in_specs=[
            pl.BlockSpec((tm, PEER_SLOTS), row),
            pl.BlockSpec((tm, PEER_SLOTS), row),
            pl.BlockSpec((tm, PEER_SLOTS), row),
            pl.BlockSpec((EC_V, D_MODEL), lambda i, j: (j, 0)),
            pl.BlockSpec((tm, D_MODEL), row),
        ],
        out_specs=pl.BlockSpec((tm, D_MODEL), row),
        scratch_shapes=[
            pltpu.VMEM((PEER_KEYS * tm, PEER_KEYS), F32),
            pltpu.VMEM((tm, D_MODEL), F32),
        ],
        compiler_params=pltpu.CompilerParams(
            dimension_semantics=("arbitrary", "arbitrary"), vmem_limit_bytes=VMEM_LIMIT),
        name="peer_v",
    )(e1, e2, w, v_bf16, x1)


def _rope_tables(seq):
    inv = ROPE_THETA ** (-jnp.arange(0, ROPE_DIM, 2, dtype=F32) / ROPE_DIM)
    ang = jnp.arange(seq, dtype=F32)[:, None] * inv[None, :]
    cos = jnp.cos(ang)
    sin = jnp.sin(ang)
    lane = np.arange(LANES)
    in_head = lane % HEAD_DIM
    freq = in_head % ROPE_HALF
    cos_l = jnp.where(in_head < ROPE_DIM, cos[:, freq], 1.0)
    sin_l = jnp.where(in_head < ROPE_HALF, -sin[:, freq], jnp.where(in_head < ROPE_DIM, sin[:, freq], 0.0))
    return cos_l.astype(F32), sin_l.astype(F32)


def _layer(x, p):
    batch, seq, _ = x.shape
    tokens = batch * seq
    x2d = x.reshape(tokens, D_MODEL)
    cos_t, sin_t = _rope_tables(seq)
    qa, ka, va, qb, kb, vb, gates = _inproj(
        x2d, seq, p["norm1"], p["w_in"], p["gqa"], p["gka"], p["gqb"], p["gkb"], cos_t, sin_t, p["bd"])

    oa, = _banded_attention(
        qa.reshape(batch, seq, A_Q_W), ka.reshape(batch, seq, A_KV_W), va.reshape(batch, seq, A_KV_W),
        window=A_WINDOW, dil=1, q_block=lambda r: 0, kv_block=lambda r: 0, nkv=A_KV_HEADS, grp=A_GROUP,
        sink=p["sink"], name="attn_a")

    o_groups, lse_groups = [], []
    for gi, (win, dil) in enumerate(B_PATTERNS):
        sub = seq // dil
        blk = functools.partial(lambda r, gi: r * B_GROUPS + gi, gi=gi)
        o, lse = _banded_attention(
            qb.reshape(batch, sub, dil * B_W), kb.reshape(batch, sub, dil * B_W), vb.reshape(batch, sub, dil * B_W),
            window=win // (2 * dil), dil=dil, q_block=blk, kv_block=blk, nkv=B_HEADS_PER_GROUP, grp=1,
            want_lse=True, out_dtype=F32, name=f"attn_b{gi}")
        o_groups.append(o.reshape(tokens, B_GROUP_W))
        lse_groups.append(lse.reshape(tokens, B_GROUP_W))

    x1, h2, qp = _post(oa.reshape(tokens, A_Q_W), o_groups, lse_groups, gates, x2d,
                       p["w_o_a"], p["w_o_b"], p["w_out"], p["norm2"], p["w_query"])
    e1, e2, gate = _topk(qp, p["k1"], p["k2"])
    w = _peer_u(h2, p["u_t"], e1, e2, gate)
    y = _peer_v(e1, e2, w, p["v"], x1)
    return y.reshape(batch, seq, D_MODEL)


def _tile_gain(g):
    return jnp.tile(g.astype(F32), LANES // HEAD_DIM).reshape(1, LANES)


def kernel(x_prompt, x_sample, norm1, w_in, q_norm_a, k_norm_a, sink_a, q_norm_b, k_norm_b, w_o_a, w_o_b, w_out,
           norm2, w_query, sub_keys_1, sub_keys_2, expert_u, expert_v):
    depth = norm1.shape[0]
    head_of_lane = np.arange(PREP_COLS) // HEAD_DIM
    bd = jnp.asarray((head_of_lane[:, None] == head_of_lane[None, :]) / HEAD_DIM, dtype=BF16)
    params = []
    for l in range(depth):
        params.append({
            "norm1": norm1[l].reshape(1, D_MODEL),
            "w_in": w_in[l].astype(BF16),
            "gqa": _tile_gain(q_norm_a[l]), "gka": _tile_gain(k_norm_a[l]),
            "gqb": _tile_gain(q_norm_b[l]), "gkb": _tile_gain(k_norm_b[l]),
            "sink": sink_a[l].astype(F32),
            "w_o_a": w_o_a[l].astype(BF16), "w_o_b": w_o_b[l].astype(BF16), "w_out": w_out[l].astype(BF16),
            "norm2": norm2[l].reshape(1, D_MODEL),
            "w_query": w_query[l].astype(BF16),
            "k1": sub_keys_1[l].astype(BF16), "k2": sub_keys_2[l].astype(BF16),
            "u_t": expert_u[l].astype(BF16).T,
            "v": expert_v[l].astype(BF16),
            "bd": bd,
        })
    ys = []
    for x in (x_prompt, x_sample):
        for p in params:
            x = _layer(x, p)
        ys.append(x)
    return tuple(ys)
```

```python
import functools

import numpy as np
import jax
import jax.numpy as jnp
from jax import lax
from jax.experimental import pallas as pl
from jax.experimental.pallas import tpu as pltpu

F32 = jnp.float32
BF16 = jnp.bfloat16
I32 = jnp.int32
U32 = jnp.uint32

D_MODEL = 1024
HEAD_DIM = 64
ROPE_DIM = HEAD_DIM // 4
ROPE_HALF = ROPE_DIM // 2
ROPE_THETA = 500000.0
EPS = 1e-6
A_Q_HEADS = 8
A_KV_HEADS = 2
A_GROUP = A_Q_HEADS // A_KV_HEADS
A_WINDOW = 128
B_PATTERNS = ((128, 1), (512, 4), (2048, 16))
B_HEADS_PER_GROUP = 4
B_GROUPS = len(B_PATTERNS)
A_Q_W = A_Q_HEADS * HEAD_DIM
A_KV_W = A_KV_HEADS * HEAD_DIM
B_GROUP_W = B_HEADS_PER_GROUP * HEAD_DIM
B_W = B_GROUPS * B_GROUP_W
IN_COLS = A_Q_W + 2 * A_KV_W + 3 * B_W + 2 * D_MODEL
PEER_HEADS = 8
PEER_KEYS = 128
PEER_EXPERTS = PEER_KEYS * PEER_KEYS
PEER_HALF = 128
PEER_TOPK = 16
PEER_SLOTS = PEER_HEADS * PEER_TOPK
PEER_Q_W = PEER_HEADS * 2 * PEER_HALF

LANES = 128
SUBLANES = 8
MASK_VALUE = -1e30
NEG = float(np.finfo(np.float32).min)
VMEM_LIMIT = 56 * 1024 * 1024

OFF_QA = 0
OFF_KA = OFF_QA + A_Q_W
OFF_VA = OFF_KA + A_KV_W
OFF_QB = OFF_VA + A_KV_W
OFF_KB = OFF_QB + B_W
OFF_VB = OFF_KB + B_W
OFF_GA = OFF_VB + B_W
OFF_GB = OFF_GA + D_MODEL

TM_IN = 256
PREP_COLS = 256
TM_POST = 512
TM_TOPK = 128
TOPK_HEADS_PER_ITER = 4
TM_U = 2048
EC_U = 512
GATHER_ROWS = 8
TM_V = 1024
EC_V = 1024
ATTN_TQ = 256
ATTN_SUBQ = 128

def _dot(a, b):
    return jnp.dot(a, b, preferred_element_type=F32)


def _dot_nt(a, b):
    return lax.dot_general(a, b, (((1,), (1,)), ((), ())), preferred_element_type=F32)


def _rms(x, g):
    ms = jnp.mean(x * x, axis=-1, keepdims=True)
    return (x * lax.rsqrt(ms + EPS)) * g


def _head_prep(t, gain, cosv, sinv, bd, lane_lo):
    sq = t * t
    hi = sq.astype(BF16)
    lo = (sq - hi.astype(F32)).astype(BF16)
    ms = _dot(hi, bd) + _dot(lo, bd)
    tn = (t * lax.rsqrt(ms + EPS)) * gain
    out = []
    for c in range(t.shape[1] // LANES):
        tc = tn[:, c * LANES:(c + 1) * LANES]
        partner = jnp.where(lane_lo, pltpu.roll(tc, LANES - ROPE_HALF, 1), pltpu.roll(tc, ROPE_HALF, 1))
        out.append(tc * cosv + partner * sinv)
    return out


def _inproj_body(x_ref, n1_ref, w_ref, gqa_ref, gka_ref, gqb_ref, gkb_ref, cos_ref, sin_ref, bd_ref,
                 qa_ref, ka_ref, va_ref, qb_ref, kb_ref, vb_ref, gate_ref):
    h = _rms(x_ref[...], n1_ref[...]).astype(BF16)
    cosv = cos_ref[...]
    sinv = sin_ref[...]
    bd = bd_ref[...]
    lane = lax.broadcasted_iota(I32, (1, LANES), 1)
    lane_lo = (lane % HEAD_DIM) < ROPE_HALF

    def prepped(off, width, gain_ref, scale, out_ref):
        step = min(width, PREP_COLS)
        gain = jnp.concatenate([gain_ref[...]] * (step // LANES), axis=-1)
        for c in range(width // step):
            t = _dot(h, w_ref[:, off + c * step: off + (c + 1) * step])
            for k, tk in enumerate(_head_prep(t, gain, cosv, sinv, bd[:step, :step], lane_lo)):
                if scale != 1.0:
                    tk = tk * scale
                lo = c * step + k * LANES
                out_ref[:, lo:lo + LANES] = tk.astype(out_ref.dtype)

    qscale = HEAD_DIM ** -0.5
    prepped(OFF_QA, A_Q_W, gqa_ref, qscale, qa_ref)
    prepped(OFF_KA, A_KV_W, gka_ref, 1.0, ka_ref)
    va_ref[...] = _dot(h, w_ref[:, OFF_VA:OFF_VA + A_KV_W]).astype(va_ref.dtype)
    prepped(OFF_QB, B_W, gqb_ref, qscale, qb_ref)
    prepped(OFF_KB, B_W, gkb_ref, 1.0, kb_ref)
    vb_ref[...] = _dot(h, w_ref[:, OFF_VB:OFF_VB + B_W]).astype(vb_ref.dtype)
    for c in range(2 * D_MODEL // 512):
        g = _dot(h, w_ref[:, OFF_GA + c * 512: OFF_GA + (c + 1) * 512])
        gate_ref[:, c * 512:(c + 1) * 512] = jax.nn.sigmoid(g)


def _inproj(x2d, seq, n1, w_in, gqa, gka, gqb, gkb, cos_t, sin_t, bd):
    tokens = x2d.shape[0]
    tm = TM_IN
    blocks_per_seq = seq // tm
    row = lambda i: (i, 0)
    const = lambda i: (0, 0)
    pos = lambda i: (i % blocks_per_seq, 0)
    out_shape = (
        jax.ShapeDtypeStruct((tokens, A_Q_W), BF16),
        jax.ShapeDtypeStruct((tokens, A_KV_W), BF16),
        jax.ShapeDtypeStruct((tokens, A_KV_W), BF16),
        jax.ShapeDtypeStruct((tokens, B_W), BF16),
        jax.ShapeDtypeStruct((tokens, B_W), BF16),
        jax.ShapeDtypeStruct((tokens, B_W), BF16),
        jax.ShapeDtypeStruct((tokens, 2 * D_MODEL), F32),
    )
    return pl.pallas_call(
        _inproj_body,
        out_shape=out_shape,
        grid=(tokens // tm,),
        in_specs=[
            pl.BlockSpec((tm, D_MODEL), row),
            pl.BlockSpec((1, D_MODEL), const),
            pl.BlockSpec((D_MODEL, IN_COLS), const),
            pl.BlockSpec((1, LANES), const),
            pl.BlockSpec((1, LANES), const),
            pl.BlockSpec((1, LANES), const),
            pl.BlockSpec((1, LANES), const),
            pl.BlockSpec((tm, LANES), pos),
            pl.BlockSpec((tm, LANES), pos),
            pl.BlockSpec((PREP_COLS, PREP_COLS), const),
        ],
        out_specs=(
            pl.BlockSpec((tm, A_Q_W), row),
            pl.BlockSpec((tm, A_KV_W), row),
            pl.BlockSpec((tm, A_KV_W), row),
            pl.BlockSpec((tm, B_W), row),
            pl.BlockSpec((tm, B_W), row),
            pl.BlockSpec((tm, B_W), row),
            pl.BlockSpec((tm, 2 * D_MODEL), row),
        ),
        compiler_params=pltpu.CompilerParams(
            dimension_semantics=("arbitrary",), vmem_limit_bytes=VMEM_LIMIT),
        name="inproj",
    )(x2d, n1, w_in, gqa, gka, gqb, gkb, cos_t, sin_t, bd)


def _attn_body(*refs, window, tq, kw, length, nkv, grp, has_sink, want_lse):
    refs = list(refs)
    sink_ref = refs.pop(0) if has_sink else None
    q_ref, k_ref, v_ref, o_ref = refs[:4]
    lse_ref = refs[4] if want_lse else None

    q0 = pl.program_id(2) * tq
    if kw == length:
        start = 0
        kwin = k_ref[...]
        vwin = v_ref[...]
    else:
        start = pl.multiple_of(jnp.clip(q0 - window, 0, length - kw), window)
        kwin = k_ref[pl.ds(start, kw), :]
        vwin = v_ref[pl.ds(start, kw), :]
    sq = min(tq, ATTN_SUBQ) if grp > 1 else tq
    qi = lax.broadcasted_iota(I32, (sq, kw), 0)
    kj = lax.broadcasted_iota(I32, (sq, kw), 1)
    rel = kj - qi + (start - q0)
    valid = [jnp.abs(rel - r0) <= window for r0 in range(0, tq, sq)]

    o_parts = [None] * (nkv * grp)
    lse_parts = [None] * (nkv * grp)
    for j in range(nkv):
        kh = kwin[:, j * HEAD_DIM:(j + 1) * HEAD_DIM]
        vh = vwin[:, j * HEAD_DIM:(j + 1) * HEAD_DIM]
        for g in range(grp):
            h = j * grp + g
            o_rows, lse_rows = [], []
            for i, r0 in enumerate(range(0, tq, sq)):
                qh = q_ref[r0:r0 + sq, h * HEAD_DIM:(h + 1) * HEAD_DIM]
                s = jnp.where(valid[i], _dot_nt(qh, kh), MASK_VALUE)
                m = jnp.max(s, axis=-1, keepdims=True)
                if has_sink:
                    m = jnp.maximum(m, sink_ref[h])
                p = jnp.exp(s - m)
                denom = jnp.sum(p, axis=-1, keepdims=True)
                if has_sink:
                    denom = denom + jnp.exp(sink_ref[h] - m)
                o_rows.append(_dot(p.astype(BF16), vh) / denom)
                if want_lse:
                    lse_rows.append(jnp.broadcast_to(m + jnp.log(denom), (sq, HEAD_DIM)))
            o_parts[h] = o_rows[0] if len(o_rows) == 1 else jnp.concatenate(o_rows, axis=0)
            if want_lse:
                lse_parts[h] = lse_rows[0] if len(lse_rows) == 1 else jnp.concatenate(lse_rows, axis=0)
    o_ref[...] = jnp.concatenate(o_parts, axis=-1).astype(o_ref.dtype)
    if want_lse:
        lse_ref[...] = jnp.concatenate(lse_parts, axis=-1)


def _banded_attention(q, k, v, *, window, dil, q_block, kv_block, nkv, grp, sink=None, want_lse=False,
                      out_dtype=BF16, name):
    batch, length, _ = q.shape
    wq = nkv * grp * HEAD_DIM
    wk = nkv * HEAD_DIM
    tq = min(ATTN_TQ, length)
    kw = min(length, tq + 2 * window)
    out_cols = dil * wq
    body = functools.partial(_attn_body, window=window, tq=tq, kw=kw, length=length, nkv=nkv, grp=grp,
                             has_sink=sink is not None, want_lse=want_lse)
    in_specs = [
        pl.BlockSpec((None, tq, wq), lambda b, r, t: (b, t, q_block(r))),
        pl.BlockSpec((None, length, wk), lambda b, r, t: (b, 0, kv_block(r))),
        pl.BlockSpec((None, length, wk), lambda b, r, t: (b, 0, kv_block(r))),
    ]
    args = [q, k, v]
    if sink is not None:
        in_specs.insert(0, pl.BlockSpec(memory_space=pltpu.SMEM))
        args.insert(0, sink)
    o_spec = pl.BlockSpec((None, tq, wq), lambda b, r, t: (b, t, r))
    out_shape = [jax.ShapeDtypeStruct((batch, length, out_cols), out_dtype)]
    out_specs = [o_spec]
    if want_lse:
        out_shape.append(jax.ShapeDtypeStruct((batch, length, out_cols), F32))
        out_specs.append(o_spec)
    return pl.pallas_call(
        body,
        out_shape=tuple(out_shape),
        grid=(batch, dil, length // tq),
        in_specs=in_specs,
        out_specs=tuple(out_specs),
        compiler_params=pltpu.CompilerParams(
            dimension_semantics=("arbitrary", "arbitrary", "arbitrary"), vmem_limit_bytes=VMEM_LIMIT),
        name=name,
    )(*args)


def _post_body(oa_ref, o1_ref, o2_ref, o3_ref, l1_ref, l2_ref, l3_ref, ga_ref, gb_ref, x_ref,
               woa_ref, wob_ref, wout_ref, n2_ref, wq_ref,
               x1_ref, h2_ref, qp_ref):
    l1, l2, l3 = l1_ref[...], l2_ref[...], l3_ref[...]
    mx = jnp.maximum(jnp.maximum(l1, l2), l3)
    e1, e2, e3 = jnp.exp(l1 - mx), jnp.exp(l2 - mx), jnp.exp(l3 - mx)
    tot = e1 + e2 + e3
    ob = (e1 / tot) * o1_ref[...] + (e2 / tot) * o2_ref[...] + (e3 / tot) * o3_ref[...]
    ya = _dot(oa_ref[...], woa_ref[...])
    yb = _dot(ob.astype(BF16), wob_ref[...])
    merged = ga_ref[...] * ya + gb_ref[...] * yb
    x1 = x_ref[...] + _dot(merged.astype(BF16), wout_ref[...])
    x1_ref[...] = x1
    h2 = _rms(x1, n2_ref[...]).astype(BF16)
    h2_ref[...] = h2
    for c in range(PEER_Q_W // PEER_HALF):
        qp_ref[c] = _dot(h2, wq_ref[:, c * PEER_HALF:(c + 1) * PEER_HALF]).astype(qp_ref.dtype)


def _post(oa, o_groups, lse_groups, gates, x2d, w_o_a, w_o_b, w_out, n2, w_query):
    tokens = x2d.shape[0]
    tm = TM_POST
    row = lambda i: (i, 0)
    const = lambda i: (0, 0)
    nq = PEER_Q_W // PEER_HALF
    return pl.pallas_call(
        _post_body,
        out_shape=(
            jax.ShapeDtypeStruct((tokens, D_MODEL), F32),
            jax.ShapeDtypeStruct((tokens, D_MODEL), BF16),
            jax.ShapeDtypeStruct((nq, tokens, PEER_HALF), BF16),
        ),
        grid=(tokens // tm,),
        in_specs=[pl.BlockSpec((tm, A_Q_W), row)]
        + [pl.BlockSpec((tm, B_GROUP_W), row)] * (2 * B_GROUPS)
        + [
            pl.BlockSpec((tm, D_MODEL), lambda i: (i, 0)),
            pl.BlockSpec((tm, D_MODEL), lambda i: (i, 1)),
            pl.BlockSpec((tm, D_MODEL), row),
            pl.BlockSpec((A_Q_W, D_MODEL), const),
            pl.BlockSpec((B_GROUP_W, D_MODEL), const),
            pl.BlockSpec((D_MODEL, D_MODEL), const),
            pl.BlockSpec((1, D_MODEL), const),
            pl.BlockSpec((D_MODEL, PEER_Q_W), const),
        ],
        out_specs=(
            pl.BlockSpec((tm, D_MODEL), row),
            pl.BlockSpec((tm, D_MODEL), row),
            pl.BlockSpec((nq, tm, PEER_HALF), lambda i: (0, i, 0)),
        ),
        compiler_params=pltpu.CompilerParams(
            dimension_semantics=("arbitrary",), vmem_limit_bytes=VMEM_LIMIT),
        name="post",
    )(oa, *o_groups, *lse_groups, gates, gates, x2d, w_o_a, w_o_b, w_out, n2, w_query)


def _tree(xs, op):
    xs = list(xs)
    while len(xs) > 1:
        nxt = [op(xs[i], xs[i + 1]) for i in range(0, len(xs) - 1, 2)]
        if len(xs) % 2:
            nxt.append(xs[-1])
        xs = nxt
    return xs[0]


def _all_rows(blocks, op):
    r = _tree(blocks, op)
    for shift in (4, 2, 1):
        r = op(r, pltpu.roll(r, shift, 0))
    return r


def _pack(rows, sub):
    blk = rows[0]
    for i in range(1, SUBLANES):
        blk = jnp.where(sub == i, rows[i], blk)
    return blk


def _select_top(blocks, ids, payload, k):
    past = float(np.finfo(np.float32).max)
    out = []
    for _ in range(k):
        m = _all_rows(blocks, jnp.maximum)
        pos = _all_rows([jnp.where(b == m, i, past) for b, i in zip(blocks, ids)], jnp.minimum)
        sel = [i == pos for i in ids]
        picked = None
        if payload is not None:
            picked = _all_rows([jnp.where(s, p, -1.0) for s, p in zip(sel, payload)], jnp.maximum)
        blocks = [jnp.where(s, NEG, b) for s, b in zip(sel, blocks)]
        out.append((m, pos, picked))
    return out


def _candidate_rows(first, second, combine, sub):
    half = SUBLANES
    first_hi = pltpu.roll(_pack(first[half:], sub), 4, 0)
    lo = _pack(second[:half], sub)
    hi = _pack(second[half:], sub)
    w = jnp.where
    a_blocks = [first[0], first[0], first[1],
                w(sub < 5, first[2], first[5]),
                w(sub < 4, first[3], first[4]),
                w(sub < 2, first[6], w(sub < 4, first[7], first_hi)),
                first_hi]
    b_blocks = [lo, hi, lo,
                w(sub < 5, lo, pltpu.roll(lo, 5, 0)),
                w(sub < 4, lo, pltpu.roll(lo, 4, 0)),
                w(sub < 2, lo, w(sub < 4, pltpu.roll(lo, 2, 0), second[0])),
                second[0]]
    t = PEER_TOPK
    flat = [sub, half + sub, t + sub,
            w(sub < 5, 2 * t + sub, 5 * t + sub - 5),
            w(sub < 4, 3 * t + sub, 4 * t + sub - 4),
            w(sub < 2, 6 * t + sub, w(sub < 4, 7 * t + sub - 2, (half + sub - 4) * t)),
            (half + 4 + sub) * t]
    used = [None, None, None, sub < 7, sub < 7, None, sub < 4]
    flat = [f if u is None else w(u, f, -1) for f, u in zip(flat, used)]
    return [combine(a, b) for a, b in zip(a_blocks, b_blocks)], flat


def _head_experts(s1, s2):
    tm = s1.shape[1]
    sub = lax.broadcasted_iota(I32, (SUBLANES, tm), 0)
    subf = sub.astype(F32)
    key_ids = [subf + float(SUBLANES * g) for g in range(PEER_KEYS // SUBLANES)]

    def half(s):
        top = _select_top([s[SUBLANES * g:SUBLANES * (g + 1)] for g in range(PEER_KEYS // SUBLANES)], key_ids,
                          None, PEER_TOPK)
        return [t[0] for t in top], [t[1] for t in top]

    v1, i1 = half(s1)
    v2, i2 = half(s2)
    cand, flat = _candidate_rows(v1, v2, lambda a, b: a + b, sub)
    cand = [jnp.where(f >= 0, c, NEG) for c, f in zip(cand, flat)]
    expert, _ = _candidate_rows(i1, i2, lambda a, b: a * float(PEER_KEYS) + b, sub)
    unused_base = PEER_TOPK * PEER_TOPK
    ids = [jnp.where(f >= 0, f, unused_base + SUBLANES * g + sub).astype(F32) for g, f in enumerate(flat)]
    top = _select_top(cand, ids, expert, PEER_TOPK)
    ex = [jnp.exp(_pack([t[0] for t in top[o:o + SUBLANES]], sub) - top[0][0]) for o in (0, SUBLANES)]
    tot = _all_rows(ex, jnp.add)
    out = []
    for o, e in zip((0, SUBLANES), ex):
        code = _pack([t[2] for t in top[o:o + SUBLANES]], sub)
        e1 = jnp.floor(code * (1.0 / PEER_KEYS))
        out.append((e / tot, e1, code - e1 * float(PEER_KEYS)))
    return out


def _peer_head(h, qp_ref, k1_ref, k2_ref, e1t_scr, e2t_scr, gt_scr):
    s1 = _dot_nt(k1_ref[h], qp_ref[2 * h])
    s2 = _dot_nt(k2_ref[h], qp_ref[2 * h + 1])
    base = pl.multiple_of(h * PEER_TOPK, PEER_TOPK)
    for o, (gate, e1, e2) in zip((0, SUBLANES), _head_experts(s1, s2)):
        rows = pl.ds(pl.multiple_of(base + o, SUBLANES), SUBLANES)
        gt_scr[rows, :] = gate
        e1t_scr[rows, :] = e1
        e2t_scr[rows, :] = e2


def _topk_body(qp_ref, k1_ref, k2_ref, e1_ref, e2_ref, g_ref, e1t_scr, e2t_scr, gt_scr):
    def head_group(i, carry):
        for o in range(TOPK_HEADS_PER_ITER):
            _peer_head(TOPK_HEADS_PER_ITER * i + o, qp_ref, k1_ref, k2_ref, e1t_scr, e2t_scr, gt_scr)
        return carry

    lax.fori_loop(0, PEER_HEADS // TOPK_HEADS_PER_ITER, head_group, 0)
    e1_ref[...] = e1t_scr[...].T.astype(I32)
    e2_ref[...] = e2t_scr[...].T.astype(I32)
    g_ref[...] = gt_scr[...].T


def _topk(qp, k1, k2):
    nq, tokens, _ = qp.shape
    tm = TM_TOPK
    assert tm == PEER_SLOTS
    row = lambda i: (i, 0)
    const3 = lambda i: (0, 0, 0)
    return pl.pallas_call(
        _topk_body,
        out_shape=(
            jax.ShapeDtypeStruct((tokens, PEER_SLOTS), I32),
            jax.ShapeDtypeStruct((tokens, PEER_SLOTS), I32),
            jax.ShapeDtypeStruct((tokens, PEER_SLOTS), F32),
        ),
        grid=(tokens // tm,),
        in_specs=[
            pl.BlockSpec((nq, tm, PEER_HALF), lambda i: (0, i, 0)),
            pl.BlockSpec((PEER_HEADS, PEER_KEYS, PEER_HALF), const3),
            pl.BlockSpec((PEER_HEADS, PEER_KEYS, PEER_HALF), const3),
        ],
        out_specs=(
            pl.BlockSpec((tm, PEER_SLOTS), row),
            pl.BlockSpec((tm, PEER_SLOTS), row),
            pl.BlockSpec((tm, PEER_SLOTS), row),
        ),
        scratch_shapes=[
            pltpu.VMEM((PEER_SLOTS, tm), F32),
            pltpu.VMEM((PEER_SLOTS, tm), F32),
            pltpu.VMEM((PEER_SLOTS, tm), F32),
        ],
        compiler_params=pltpu.CompilerParams(
            dimension_semantics=("arbitrary",), vmem_limit_bytes=VMEM_LIMIT),
        name="topk",
    )(qp, k1, k2)


def _peer_u_body(h2_ref, ut_ref, e1_ref, e2_ref, g_ref, w_ref, acc_ref, sc_ref):
    j = pl.program_id(1)
    n_e1 = EC_U // PEER_KEYS

    @pl.when(j == 0)
    def _():
        acc_ref[...] = jnp.zeros_like(acc_ref)
        sc_ref[...] = jnp.zeros_like(sc_ref)

    prev = sc_ref.at[(j + 1) % 2]
    for r in range(0, h2_ref.shape[0], GATHER_ROWS):
        rows = slice(r, r + GATHER_ROWS)
        e1 = e1_ref[rows, :]
        e2 = e2_ref[rows, :]
        acc = acc_ref[rows, :]
        for c in range(n_e1):
            picked = jnp.take_along_axis(prev[rows, c * PEER_KEYS:(c + 1) * PEER_KEYS], e2, axis=1)
            acc = jnp.where(e1 == (j - 1) * n_e1 + c, picked, acc)
        acc_ref[rows, :] = acc
    sc_ref[j % 2] = _dot(h2_ref[...], ut_ref[...])

    @pl.when(j == pl.num_programs(1) - 1)
    def _():
        acc = acc_ref[...]
        w_ref[...] = g_ref[...] * (0.5 * acc * (1.0 + lax.erf(acc * np.float32(np.sqrt(0.5)))))


def _peer_u(h2, u_t, e1, e2, gate):
    tokens = h2.shape[0]
    tm = TM_U
    n_chunks = PEER_EXPERTS // EC_U
    row = lambda i, j: (i, 0)
    return pl.pallas_call(
        _peer_u_body,
        out_shape=jax.ShapeDtypeStruct((tokens, PEER_SLOTS), F32),
        grid=(tokens // tm, n_chunks + 1),
        in_specs=[
            pl.BlockSpec((tm, D_MODEL), row),
            pl.BlockSpec((D_MODEL, EC_U), lambda i, j: (0, jnp.minimum(j, n_chunks - 1))),
            pl.BlockSpec((tm, PEER_SLOTS), row),
            pl.BlockSpec((tm, PEER_SLOTS), row),
            pl.BlockSpec((tm, PEER_SLOTS), row),
        ],
        out_specs=pl.BlockSpec((tm, PEER_SLOTS), row),
        scratch_shapes=[pltpu.VMEM((tm, PEER_SLOTS), F32), pltpu.VMEM((2, tm, EC_U), F32)],
        compiler_params=pltpu.CompilerParams(
            dimension_semantics=("arbitrary", "arbitrary"), vmem_limit_bytes=VMEM_LIMIT),
        name="peer_u",
    )(h2, u_t, e1, e2, gate)


def _bf16_bits(x):
    return lax.bitcast_convert_type(x.astype(BF16).astype(F32), U32)


def _peer_v_body(e1_ref, e2_ref, w_ref, v_ref, x1_ref, y_ref, wd_scr):
    tm = e1_ref.shape[0]
    half = tm // 2
    j = pl.program_id(1)

    @pl.when(j == 0)
    def _():
        eiota = lax.broadcasted_iota(I32, (PEER_KEYS, PEER_SLOTS), 0)

        def tile(t):
            e1 = e1_ref[pl.ds(t, 1), :]
            e2 = e2_ref[pl.ds(t, 1), :]
            wt = w_ref[pl.ds(t, 1), :]
            a = jnp.where(eiota == e1, wt, 0.0).astype(BF16)
            b = jnp.where(eiota == e2, 1.0, 0.0).astype(BF16)
            return _dot_nt(a, b)

        def token_pair(p, carry):
            packed = (_bf16_bits(tile(p)) >> 16) | _bf16_bits(tile(p + half))
            wd_scr[pl.ds(p, PEER_KEYS, stride=half), :] = packed
            return carry

        lax.fori_loop(0, half, token_pair, 0, unroll=8)
        y_ref[...] = x1_ref[...]

    n_e1 = EC_V // PEER_KEYS
    parts = []
    for c in range(n_e1):
        words = wd_scr[pl.ds(pl.multiple_of((j * n_e1 + c) * half, half), half), :]
        lo = lax.bitcast_convert_type(words << 16, F32)
        hi = lax.bitcast_convert_type(words & jnp.uint32(0xFFFF0000), F32)
        parts.append(jnp.concatenate([lo, hi], axis=0))
    dense_w = jnp.concatenate(parts, axis=-1).astype(BF16)
    y_ref[...] += _dot(dense_w, v_ref[...])


def _peer_v(e1, e2, w, v_bf16, x1):
    tokens = x1.shape[0]
    tm = TM_V
    row = lambda i, j: (i, 0)
    return pl.pallas_call(
        _peer_v_body,
        out_shape=jax.ShapeDtypeStruct((tokens, D_MODEL), F32),
        grid=(tokens // tm, PEER_EXPERTS // EC_V),
        in_specs=[
            pl.BlockSpec((tm, PEER_SLOTS), row),
            pl.BlockSpec((tm, PEER_SLOTS), row),
            pl.BlockSpec((tm, PEER_SLOTS), row),
            pl.BlockSpec((EC_V, D_MODEL), lambda i, j: (j, 0)),
            pl.BlockSpec((tm, D_MODEL), row, pipeline_mode=pl.Buffered(1)),
        ],
        out_specs=pl.BlockSpec((tm, D_MODEL), row, pipeline_mode=pl.Buffered(1)),
        scratch_shapes=[pltpu.VMEM((PEER_KEYS * tm // 2, PEER_KEYS), U32)],
        compiler_params=pltpu.CompilerParams(
            dimension_semantics=("arbitrary", "arbitrary"), vmem_limit_bytes=VMEM_LIMIT),
        name="peer_v",
    )(e1, e2, w, v_bf16, x1)


def _rope_tables(seq):
    inv = ROPE_THETA ** (-jnp.arange(0, ROPE_DIM, 2, dtype=F32) / ROPE_DIM)
    ang = jnp.arange(seq, dtype=F32)[:, None] * inv[None, :]
    cos = jnp.cos(ang)
    sin = jnp.sin(ang)
    lane = np.arange(LANES)
    in_head = lane % HEAD_DIM
    freq = in_head % ROPE_HALF
    cos_l = jnp.where(in_head < ROPE_DIM, cos[:, freq], 1.0)
    sin_l = jnp.where(in_head < ROPE_HALF, -sin[:, freq], jnp.where(in_head < ROPE_DIM, sin[:, freq], 0.0))
    return cos_l.astype(F32), sin_l.astype(F32)


def _layer(x, p):
    batch, seq, _ = x.shape
    tokens = batch * seq
    x2d = x.reshape(tokens, D_MODEL)
    cos_t, sin_t = _rope_tables(seq)
    qa, ka, va, qb, kb, vb, gates = _inproj(
        x2d, seq, p["norm1"], p["w_in"], p["gqa"], p["gka"], p["gqb"], p["gkb"], cos_t, sin_t, p["bd"])

    oa, = _banded_attention(
        qa.reshape(batch, seq, A_Q_W), ka.reshape(batch, seq, A_KV_W), va.reshape(batch, seq, A_KV_W),
        window=A_WINDOW, dil=1, q_block=lambda r: 0, kv_block=lambda r: 0, nkv=A_KV_HEADS, grp=A_GROUP,
        sink=p["sink"], name="attn_a")

    o_groups, lse_groups = [], []
    for gi, (win, dil) in enumerate(B_PATTERNS):
        sub = seq // dil
        blk = functools.partial(lambda r, gi: r * B_GROUPS + gi, gi=gi)
        o, lse = _banded_attention(
            qb.reshape(batch, sub, dil * B_W), kb.reshape(batch, sub, dil * B_W), vb.reshape(batch, sub, dil * B_W),
            window=win // (2 * dil), dil=dil, q_block=blk, kv_block=blk, nkv=B_HEADS_PER_GROUP, grp=1,
            want_lse=True, out_dtype=F32, name=f"attn_b{gi}")
        o_groups.append(o.reshape(tokens, B_GROUP_W))
        lse_groups.append(lse.reshape(tokens, B_GROUP_W))

    x1, h2, qp = _post(oa.reshape(tokens, A_Q_W), o_groups, lse_groups, gates, x2d,
                       p["w_o_a"], p["w_o_b"], p["w_out"], p["norm2"], p["w_query"])
    e1, e2, gate = _topk(qp, p["k1"], p["k2"])
    w = _peer_u(h2, p["u_t"], e1, e2, gate)
    y = _peer_v(e1, e2, w, p["v"], x1)
    return y.reshape(batch, seq, D_MODEL)


def _tile_gain(g):
    return jnp.tile(g.astype(F32), LANES // HEAD_DIM).reshape(1, LANES)


def kernel(x_prompt, x_sample, norm1, w_in, q_norm_a, k_norm_a, sink_a, q_norm_b, k_norm_b, w_o_a, w_o_b, w_out,
           norm2, w_query, sub_keys_1, sub_keys_2, expert_u, expert_v):
    depth = norm1.shape[0]
    head_of_lane = np.arange(PREP_COLS) // HEAD_DIM
    bd = jnp.asarray((head_of_lane[:, None] == head_of_lane[None, :]) / HEAD_DIM, dtype=BF16)
    params = []
    for l in range(depth):
        params.append({
            "norm1": norm1[l].reshape(1, D_MODEL),
            "w_in": w_in[l].astype(BF16),
            "gqa": _tile_gain(q_norm_a[l]), "gka": _tile_gain(k_norm_a[l]),
            "gqb": _tile_gain(q_norm_b[l]), "gkb": _tile_gain(k_norm_b[l]),
            "sink": sink_a[l].astype(F32),
            "w_o_a": w_o_a[l].astype(BF16), "w_o_b": w_o_b[l].astype(BF16), "w_out": w_out[l].astype(BF16),
            "norm2": norm2[l].reshape(1, D_MODEL),
            "w_query": w_query[l].astype(BF16),
            "k1": sub_keys_1[l].astype(BF16), "k2": sub_keys_2[l].astype(BF16),
            "u_t": expert_u[l].astype(BF16).T,
            "v": expert_v[l].astype(BF16),
            "bd": bd,
        })
    ys = []
    for x in (x_prompt, x_sample):
        for p in params:
            x = _layer(x, p)
        ys.append(x)
    return tuple(ys)
```

```python
import functools

import numpy as np
import jax
import jax.numpy as jnp
from jax import lax
from jax.experimental import pallas as pl
from jax.experimental.pallas import tpu as pltpu

F32 = jnp.float32
BF16 = jnp.bfloat16
I32 = jnp.int32
U32 = jnp.uint32

D_MODEL = 1024
HEAD_DIM = 64
ROPE_DIM = HEAD_DIM // 4
ROPE_HALF = ROPE_DIM // 2
ROPE_THETA = 500000.0
EPS = 1e-6
A_Q_HEADS = 8
A_KV_HEADS = 2
A_GROUP = A_Q_HEADS // A_KV_HEADS
A_WINDOW = 128
B_PATTERNS = ((128, 1), (512, 4), (2048, 16))
B_HEADS_PER_GROUP = 4
B_GROUPS = len(B_PATTERNS)
A_Q_W = A_Q_HEADS * HEAD_DIM
A_KV_W = A_KV_HEADS * HEAD_DIM
B_GROUP_W = B_HEADS_PER_GROUP * HEAD_DIM
B_W = B_GROUPS * B_GROUP_W
IN_COLS = A_Q_W + 2 * A_KV_W + 3 * B_W + 2 * D_MODEL
PEER_HEADS = 8
PEER_KEYS = 128
PEER_EXPERTS = PEER_KEYS * PEER_KEYS
PEER_HALF = 128
PEER_TOPK = 16
PEER_SLOTS = PEER_HEADS * PEER_TOPK
PEER_Q_W = PEER_HEADS * 2 * PEER_HALF

LANES = 128
SUBLANES = 8
MASK_VALUE = -1e30
NEG = float(np.finfo(np.float32).min)
VMEM_LIMIT = 56 * 1024 * 1024

OFF_QA = 0
OFF_KA = OFF_QA + A_Q_W
OFF_VA = OFF_KA + A_KV_W
OFF_QB = OFF_VA + A_KV_W
OFF_KB = OFF_QB + B_W
OFF_VB = OFF_KB + B_W
OFF_GA = OFF_VB + B_W
OFF_GB = OFF_GA + D_MODEL

TM_IN = 512
PREP_COLS = 256
TM_POST = 512
TM_TOPK = 128
TOPK_HEADS_PER_ITER = 4
TM_U = 2048
EC_U = 512
GATHER_ROWS = 8
TM_V = 1024
EC_V = 1024
ATTN_TQ = 256
ATTN_SUBQ = 128

def _dot(a, b):
    return jnp.dot(a, b, preferred_element_type=F32)


def _dot_nt(a, b):
    return lax.dot_general(a, b, (((1,), (1,)), ((), ())), preferred_element_type=F32)


def _rms(x, g):
    ms = jnp.mean(x * x, axis=-1, keepdims=True)
    return (x * lax.rsqrt(ms + EPS)) * g


def _head_prep(t, gain, cosv, sinv, bd, lane_lo):
    sq = t * t
    hi = sq.astype(BF16)
    lo = (sq - hi.astype(F32)).astype(BF16)
    ms = _dot(hi, bd) + _dot(lo, bd)
    tn = (t * lax.rsqrt(ms + EPS)) * gain
    out = []
    for c in range(t.shape[1] // LANES):
        tc = tn[:, c * LANES:(c + 1) * LANES]
        partner = jnp.where(lane_lo, pltpu.roll(tc, LANES - ROPE_HALF, 1), pltpu.roll(tc, ROPE_HALF, 1))
        out.append(tc * cosv + partner * sinv)
    return out


def _inproj_body(x_ref, n1_ref, w_ref, gqa_ref, gka_ref, gqb_ref, gkb_ref, cos_ref, sin_ref, bd_ref,
                 qa_ref, ka_ref, va_ref, qb_ref, kb_ref, vb_ref, gate_ref):
    h = _rms(x_ref[...], n1_ref[...]).astype(BF16)
    cosv = cos_ref[...]
    sinv = sin_ref[...]
    bd = bd_ref[...]
    lane = lax.broadcasted_iota(I32, (1, LANES), 1)
    lane_lo = (lane % HEAD_DIM) < ROPE_HALF

    def prepped(off, width, gain_ref, scale, out_ref):
        step = min(width, PREP_COLS)
        gain = jnp.concatenate([gain_ref[...]] * (step // LANES), axis=-1)
        for c in range(width // step):
            t = _dot(h, w_ref[:, off + c * step: off + (c + 1) * step])
            for k, tk in enumerate(_head_prep(t, gain, cosv, sinv, bd[:step, :step], lane_lo)):
                if scale != 1.0:
                    tk = tk * scale
                lo = c * step + k * LANES
                out_ref[:, lo:lo + LANES] = tk.astype(out_ref.dtype)

    qscale = HEAD_DIM ** -0.5
    prepped(OFF_QA, A_Q_W, gqa_ref, qscale, qa_ref)
    prepped(OFF_KA, A_KV_W, gka_ref, 1.0, ka_ref)
    va_ref[...] = _dot(h, w_ref[:, OFF_VA:OFF_VA + A_KV_W]).astype(va_ref.dtype)
    prepped(OFF_QB, B_W, gqb_ref, qscale, qb_ref)
    prepped(OFF_KB, B_W, gkb_ref, 1.0, kb_ref)
    vb_ref[...] = _dot(h, w_ref[:, OFF_VB:OFF_VB + B_W]).astype(vb_ref.dtype)
    for c in range(2 * D_MODEL // 512):
        g = _dot(h, w_ref[:, OFF_GA + c * 512: OFF_GA + (c + 1) * 512])
        gate_ref[:, c * 512:(c + 1) * 512] = jax.nn.sigmoid(g)


def _inproj(x2d, seq, n1, w_in, gqa, gka, gqb, gkb, cos_t, sin_t, bd):
    tokens = x2d.shape[0]
    tm = TM_IN
    blocks_per_seq = seq // tm
    row = lambda i: (i, 0)
    const = lambda i: (0, 0)
    pos = lambda i: (i % blocks_per_seq, 0)
    out_shape = (
        jax.ShapeDtypeStruct((tokens, A_Q_W), BF16),
        jax.ShapeDtypeStruct((tokens, A_KV_W), BF16),
        jax.ShapeDtypeStruct((tokens, A_KV_W), BF16),
        jax.ShapeDtypeStruct((tokens, B_W), BF16),
        jax.ShapeDtypeStruct((tokens, B_W), BF16),
        jax.ShapeDtypeStruct((tokens, B_W), BF16),
        jax.ShapeDtypeStruct((tokens, 2 * D_MODEL), F32),
    )
    return pl.pallas_call(
        _inproj_body,
        out_shape=out_shape,
        grid=(tokens // tm,),
        in_specs=[
            pl.BlockSpec((tm, D_MODEL), row),
            pl.BlockSpec((1, D_MODEL), const),
            pl.BlockSpec((D_MODEL, IN_COLS), const),
            pl.BlockSpec((1, LANES), const),
            pl.BlockSpec((1, LANES), const),
            pl.BlockSpec((1, LANES), const),
            pl.BlockSpec((1, LANES), const),
            pl.BlockSpec((tm, LANES), pos),
            pl.BlockSpec((tm, LANES), pos),
            pl.BlockSpec((PREP_COLS, PREP_COLS), const),
        ],
        out_specs=(
            pl.BlockSpec((tm, A_Q_W), row),
            pl.BlockSpec((tm, A_KV_W), row),
            pl.BlockSpec((tm, A_KV_W), row),
            pl.BlockSpec((tm, B_W), row),
            pl.BlockSpec((tm, B_W), row),
            pl.BlockSpec((tm, B_W), row),
            pl.BlockSpec((tm, 2 * D_MODEL), row),
        ),
        compiler_params=pltpu.CompilerParams(
            dimension_semantics=("arbitrary",), vmem_limit_bytes=VMEM_LIMIT),
        name="inproj",
    )(x2d, n1, w_in, gqa, gka, gqb, gkb, cos_t, sin_t, bd)


def _attn_body(*refs, window, tq, kw, length, nkv, grp, has_sink, want_lse):
    refs = list(refs)
    sink_ref = refs.pop(0) if has_sink else None
    q_ref, k_ref, v_ref, o_ref = refs[:4]
    lse_ref = refs[4] if want_lse else None

    q0 = pl.program_id(2) * tq
    if kw == length:
        start = 0
        kwin = k_ref[...]
        vwin = v_ref[...]
    else:
        start = pl.multiple_of(jnp.clip(q0 - window, 0, length - kw), window)
        kwin = k_ref[pl.ds(start, kw), :]
        vwin = v_ref[pl.ds(start, kw), :]
    sq = min(tq, ATTN_SUBQ) if grp > 1 else tq
    qi = lax.broadcasted_iota(I32, (sq, kw), 0)
    kj = lax.broadcasted_iota(I32, (sq, kw), 1)
    rel = kj - qi + (start - q0)
    valid = [jnp.abs(rel - r0) <= window for r0 in range(0, tq, sq)]

    o_parts = [None] * (nkv * grp)
    lse_parts = [None] * (nkv * grp)
    for j in range(nkv):
        kh = kwin[:, j * HEAD_DIM:(j + 1) * HEAD_DIM]
        vh = vwin[:, j * HEAD_DIM:(j + 1) * HEAD_DIM]
        for g in range(grp):
            h = j * grp + g
            o_rows, lse_rows = [], []
            for i, r0 in enumerate(range(0, tq, sq)):
                qh = q_ref[r0:r0 + sq, h * HEAD_DIM:(h + 1) * HEAD_DIM]
                s = jnp.where(valid[i], _dot_nt(qh, kh), MASK_VALUE)
                m = jnp.max(s, axis=-1, keepdims=True)
                if has_sink:
                    m = jnp.maximum(m, sink_ref[h])
                p = jnp.exp(s - m)
                denom = jnp.sum(p, axis=-1, keepdims=True)
                if has_sink:
                    denom = denom + jnp.exp(sink_ref[h] - m)
                o_rows.append(_dot(p.astype(BF16), vh) / denom)
                if want_lse:
                    lse_rows.append(jnp.broadcast_to(m + jnp.log(denom), (sq, HEAD_DIM)))
            o_parts[h] = o_rows[0] if len(o_rows) == 1 else jnp.concatenate(o_rows, axis=0)
            if want_lse:
                lse_parts[h] = lse_rows[0] if len(lse_rows) == 1 else jnp.concatenate(lse_rows, axis=0)
    o_ref[...] = jnp.concatenate(o_parts, axis=-1).astype(o_ref.dtype)
    if want_lse:
        lse_ref[...] = jnp.concatenate(lse_parts, axis=-1)


def _banded_attention(q, k, v, *, window, dil, q_block, kv_block, nkv, grp, sink=None, want_lse=False,
                      out_dtype=BF16, name):
    batch, length, _ = q.shape
    wq = nkv * grp * HEAD_DIM
    wk = nkv * HEAD_DIM
    tq = min(ATTN_TQ, length)
    kw = min(length, tq + 2 * window)
    out_cols = dil * wq
    body = functools.partial(_attn_body, window=window, tq=tq, kw=kw, length=length, nkv=nkv, grp=grp,
                             has_sink=sink is not None, want_lse=want_lse)
    in_specs = [
        pl.BlockSpec((None, tq, wq), lambda b, r, t: (b, t, q_block(r))),
        pl.BlockSpec((None, length, wk), lambda b, r, t: (b, 0, kv_block(r))),
        pl.BlockSpec((None, length, wk), lambda b, r, t: (b, 0, kv_block(r))),
    ]
    args = [q, k, v]
    if sink is not None:
        in_specs.insert(0, pl.BlockSpec(memory_space=pltpu.SMEM))
        args.insert(0, sink)
    o_spec = pl.BlockSpec((None, tq, wq), lambda b, r, t: (b, t, r))
    out_shape = [jax.ShapeDtypeStruct((batch, length, out_cols), out_dtype)]
    out_specs = [o_spec]
    if want_lse:
        out_shape.append(jax.ShapeDtypeStruct((batch, length, out_cols), F32))
        out_specs.append(o_spec)
    return pl.pallas_call(
        body,
        out_shape=tuple(out_shape),
        grid=(batch, dil, length // tq),
        in_specs=in_specs,
        out_specs=tuple(out_specs),
        compiler_params=pltpu.CompilerParams(
            dimension_semantics=("arbitrary", "arbitrary", "arbitrary"), vmem_limit_bytes=VMEM_LIMIT),
        name=name,
    )(*args)


def _post_body(oa_ref, o1_ref, o2_ref, o3_ref, l1_ref, l2_ref, l3_ref, ga_ref, gb_ref, x_ref,
               woa_ref, wob_ref, wout_ref, n2_ref, wq_ref,
               x1_ref, h2_ref, qp_ref):
    l1, l2, l3 = l1_ref[...], l2_ref[...], l3_ref[...]
    mx = jnp.maximum(jnp.maximum(l1, l2), l3)
    e1, e2, e3 = jnp.exp(l1 - mx), jnp.exp(l2 - mx), jnp.exp(l3 - mx)
    tot = e1 + e2 + e3
    ob = (e1 / tot) * o1_ref[...] + (e2 / tot) * o2_ref[...] + (e3 / tot) * o3_ref[...]
    ya = _dot(oa_ref[...], woa_ref[...])
    yb = _dot(ob.astype(BF16), wob_ref[...])
    merged = ga_ref[...] * ya + gb_ref[...] * yb
    x1 = x_ref[...] + _dot(merged.astype(BF16), wout_ref[...])
    x1_ref[...] = x1
    h2 = _rms(x1, n2_ref[...]).astype(BF16)
    h2_ref[...] = h2
    for c in range(PEER_Q_W // PEER_HALF):
        qp_ref[c] = _dot(h2, wq_ref[:, c * PEER_HALF:(c + 1) * PEER_HALF]).astype(qp_ref.dtype)


def _post(oa, o_groups, lse_groups, gates, x2d, w_o_a, w_o_b, w_out, n2, w_query):
    tokens = x2d.shape[0]
    tm = TM_POST
    row = lambda i: (i, 0)
    const = lambda i: (0, 0)
    nq = PEER_Q_W // PEER_HALF
    return pl.pallas_call(
        _post_body,
        out_shape=(
            jax.ShapeDtypeStruct((tokens, D_MODEL), F32),
            jax.ShapeDtypeStruct((tokens, D_MODEL), BF16),
            jax.ShapeDtypeStruct((nq, tokens, PEER_HALF), BF16),
        ),
        grid=(tokens // tm,),
        in_specs=[pl.BlockSpec((tm, A_Q_W), row)]
        + [pl.BlockSpec((tm, B_GROUP_W), row)] * (2 * B_GROUPS)
        + [
            pl.BlockSpec((tm, D_MODEL), lambda i: (i, 0)),
            pl.BlockSpec((tm, D_MODEL), lambda i: (i, 1)),
            pl.BlockSpec((tm, D_MODEL), row),
            pl.BlockSpec((A_Q_W, D_MODEL), const),
            pl.BlockSpec((B_GROUP_W, D_MODEL), const),
            pl.BlockSpec((D_MODEL, D_MODEL), const),
            pl.BlockSpec((1, D_MODEL), const),
            pl.BlockSpec((D_MODEL, PEER_Q_W), const),
        ],
        out_specs=(
            pl.BlockSpec((tm, D_MODEL), row),
            pl.BlockSpec((tm, D_MODEL), row),
            pl.BlockSpec((nq, tm, PEER_HALF), lambda i: (0, i, 0)),
        ),
        compiler_params=pltpu.CompilerParams(
            dimension_semantics=("arbitrary",), vmem_limit_bytes=VMEM_LIMIT),
        name="post",
    )(oa, *o_groups, *lse_groups, gates, gates, x2d, w_o_a, w_o_b, w_out, n2, w_query)


def _tree(xs, op):
    xs = list(xs)
    while len(xs) > 1:
        nxt = [op(xs[i], xs[i + 1]) for i in range(0, len(xs) - 1, 2)]
        if len(xs) % 2:
            nxt.append(xs[-1])
        xs = nxt
    return xs[0]


def _all_rows(blocks, op):
    r = _tree(blocks, op)
    for shift in (4, 2, 1):
        r = op(r, pltpu.roll(r, shift, 0))
    return r


def _pack(rows, sub):
    blk = rows[0]
    for i in range(1, SUBLANES):
        blk = jnp.where(sub == i, rows[i], blk)
    return blk


def _select_top(blocks, ids, payload, k):
    past = float(np.finfo(np.float32).max)
    out = []
    for _ in range(k):
        m = _all_rows(blocks, jnp.maximum)
        pos = _all_rows([jnp.where(b == m, i, past) for b, i in zip(blocks, ids)], jnp.minimum)
        sel = [i == pos for i in ids]
        picked = None
        if payload is not None:
            picked = _all_rows([jnp.where(s, p, -1.0) for s, p in zip(sel, payload)], jnp.maximum)
        blocks = [jnp.where(s, NEG, b) for s, b in zip(sel, blocks)]
        out.append((m, pos, picked))
    return out


def _candidate_rows(first, second, combine, sub):
    half = SUBLANES
    first_hi = pltpu.roll(_pack(first[half:], sub), 4, 0)
    lo = _pack(second[:half], sub)
    hi = _pack(second[half:], sub)
    w = jnp.where
    a_blocks = [first[0], first[0], first[1],
                w(sub < 5, first[2], first[5]),
                w(sub < 4, first[3], first[4]),
                w(sub < 2, first[6], w(sub < 4, first[7], first_hi)),
                first_hi]
    b_blocks = [lo, hi, lo,
                w(sub < 5, lo, pltpu.roll(lo, 5, 0)),
                w(sub < 4, lo, pltpu.roll(lo, 4, 0)),
                w(sub < 2, lo, w(sub < 4, pltpu.roll(lo, 2, 0), second[0])),
                second[0]]
    t = PEER_TOPK
    flat = [sub, half + sub, t + sub,
            w(sub < 5, 2 * t + sub, 5 * t + sub - 5),
            w(sub < 4, 3 * t + sub, 4 * t + sub - 4),
            w(sub < 2, 6 * t + sub, w(sub < 4, 7 * t + sub - 2, (half + sub - 4) * t)),
            (half + 4 + sub) * t]
    used = [None, None, None, sub < 7, sub < 7, None, sub < 4]
    flat = [f if u is None else w(u, f, -1) for f, u in zip(flat, used)]
    return [combine(a, b) for a, b in zip(a_blocks, b_blocks)], flat


def _head_experts(s1, s2):
    tm = s1.shape[1]
    sub = lax.broadcasted_iota(I32, (SUBLANES, tm), 0)
    subf = sub.astype(F32)
    key_ids = [subf + float(SUBLANES * g) for g in range(PEER_KEYS // SUBLANES)]

    def half(s):
        top = _select_top([s[SUBLANES * g:SUBLANES * (g + 1)] for g in range(PEER_KEYS // SUBLANES)], key_ids,
                          None, PEER_TOPK)
        return [t[0] for t in top], [t[1] for t in top]

    v1, i1 = half(s1)
    v2, i2 = half(s2)
    cand, flat = _candidate_rows(v1, v2, lambda a, b: a + b, sub)
    cand = [jnp.where(f >= 0, c, NEG) for c, f in zip(cand, flat)]
    expert, _ = _candidate_rows(i1, i2, lambda a, b: a * float(PEER_KEYS) + b, sub)
    unused_base = PEER_TOPK * PEER_TOPK
    ids = [jnp.where(f >= 0, f, unused_base + SUBLANES * g + sub).astype(F32) for g, f in enumerate(flat)]
    top = _select_top(cand, ids, expert, PEER_TOPK)
    ex = [jnp.exp(_pack([t[0] for t in top[o:o + SUBLANES]], sub) - top[0][0]) for o in (0, SUBLANES)]
    tot = _all_rows(ex, jnp.add)
    out = []
    for o, e in zip((0, SUBLANES), ex):
        code = _pack([t[2] for t in top[o:o + SUBLANES]], sub)
        e1 = jnp.floor(code * (1.0 / PEER_KEYS))
        out.append((e / tot, e1, code - e1 * float(PEER_KEYS)))
    return out


def _peer_head(h, qp_ref, k1_ref, k2_ref, e1t_scr, e2t_scr, gt_scr):
    s1 = _dot_nt(k1_ref[h], qp_ref[2 * h])
    s2 = _dot_nt(k2_ref[h], qp_ref[2 * h + 1])
    base = pl.multiple_of(h * PEER_TOPK, PEER_TOPK)
    for o, (gate, e1, e2) in zip((0, SUBLANES), _head_experts(s1, s2)):
        rows = pl.ds(pl.multiple_of(base + o, SUBLANES), SUBLANES)
        gt_scr[rows, :] = gate
        e1t_scr[rows, :] = e1
        e2t_scr[rows, :] = e2


def _topk_body(qp_ref, k1_ref, k2_ref, e1_ref, e2_ref, g_ref, e1t_scr, e2t_scr, gt_scr):
    def head_group(i, carry):
        for o in range(TOPK_HEADS_PER_ITER):
            _peer_head(TOPK_HEADS_PER_ITER * i + o, qp_ref, k1_ref, k2_ref, e1t_scr, e2t_scr, gt_scr)
        return carry

    lax.fori_loop(0, PEER_HEADS // TOPK_HEADS_PER_ITER, head_group, 0)
    e1_ref[...] = e1t_scr[...].T.astype(I32)
    e2_ref[...] = e2t_scr[...].T.astype(I32)
    g_ref[...] = gt_scr[...].T


def _topk(qp, k1, k2):
    nq, tokens, _ = qp.shape
    tm = TM_TOPK
    assert tm == PEER_SLOTS
    row = lambda i: (i, 0)
    const3 = lambda i: (0, 0, 0)
    return pl.pallas_call(
        _topk_body,
        out_shape=(
            jax.ShapeDtypeStruct((tokens, PEER_SLOTS), I32),
            jax.ShapeDtypeStruct((tokens, PEER_SLOTS), I32),
            jax.ShapeDtypeStruct((tokens, PEER_SLOTS), F32),
        ),
        grid=(tokens // tm,),
        in_specs=[
            pl.BlockSpec((nq, tm, PEER_HALF), lambda i: (0, i, 0)),
            pl.BlockSpec((PEER_HEADS, PEER_KEYS, PEER_HALF), const3),
            pl.BlockSpec((PEER_HEADS, PEER_KEYS, PEER_HALF), const3),
        ],
        out_specs=(
            pl.BlockSpec((tm, PEER_SLOTS), row),
            pl.BlockSpec((tm, PEER_SLOTS), row),
            pl.BlockSpec((tm, PEER_SLOTS), row),
        ),
        scratch_shapes=[
            pltpu.VMEM((PEER_SLOTS, tm), F32),
            pltpu.VMEM((PEER_SLOTS, tm), F32),
            pltpu.VMEM((PEER_SLOTS, tm), F32),
        ],
        compiler_params=pltpu.CompilerParams(
            dimension_semantics=("arbitrary",), vmem_limit_bytes=VMEM_LIMIT),
        name="topk",
    )(qp, k1, k2)


def _peer_u_body(h2_ref, ut_ref, e1_ref, e2_ref, g_ref, w_ref, acc_ref, sc_ref):
    j = pl.program_id(1)
    n_e1 = EC_U // PEER_KEYS

    @pl.when(j == 0)
    def _():
        acc_ref[...] = jnp.zeros_like(acc_ref)
        sc_ref[...] = jnp.zeros_like(sc_ref)

    prev = sc_ref.at[(j + 1) % 2]
    for r in range(0, h2_ref.shape[0], GATHER_ROWS):
        rows = slice(r, r + GATHER_ROWS)
        e1 = e1_ref[rows, :]
        e2 = e2_ref[rows, :]
        acc = acc_ref[rows, :]
        for c in range(n_e1):
            picked = jnp.take_along_axis(prev[rows, c * PEER_KEYS:(c + 1) * PEER_KEYS], e2, axis=1)
            acc = jnp.where(e1 == (j - 1) * n_e1 + c, picked, acc)
        acc_ref[rows, :] = acc
    sc_ref[j % 2] = _dot(h2_ref[...], ut_ref[...])

    @pl.when(j == pl.num_programs(1) - 1)
    def _():
        acc = acc_ref[...]
        w_ref[...] = g_ref[...] * (0.5 * acc * (1.0 + lax.erf(acc * np.float32(np.sqrt(0.5)))))


def _peer_u(h2, u_t, e1, e2, gate):
    tokens = h2.shape[0]
    tm = TM_U
    n_chunks = PEER_EXPERTS // EC_U
    row = lambda i, j: (i, 0)
    return pl.pallas_call(
        _peer_u_body,
        out_shape=jax.ShapeDtypeStruct((tokens, PEER_SLOTS), F32),
        grid=(tokens // tm, n_chunks + 1),
        in_specs=[
            pl.BlockSpec((tm, D_MODEL), row),
            pl.BlockSpec((D_MODEL, EC_U), lambda i, j: (0, jnp.minimum(j, n_chunks - 1))),
            pl.BlockSpec((tm, PEER_SLOTS), row),
            pl.BlockSpec((tm, PEER_SLOTS), row),
            pl.BlockSpec((tm, PEER_SLOTS), row),
        ],
        out_specs=pl.BlockSpec((tm, PEER_SLOTS), row),
        scratch_shapes=[pltpu.VMEM((tm, PEER_SLOTS), F32), pltpu.VMEM((2, tm, EC_U), F32)],
        compiler_params=pltpu.CompilerParams(
            dimension_semantics=("arbitrary", "arbitrary"), vmem_limit_bytes=VMEM_LIMIT),
        name="peer_u",
    )(h2, u_t, e1, e2, gate)


def _wd_pitch(tm):
    tiles = tm // 2 // SUBLANES
    return (tiles + 1 - tiles % 2) * SUBLANES


def _bf16_bits(x):
    return lax.bitcast_convert_type(x.astype(BF16).astype(F32), U32)


def _peer_v_body(e1_ref, e2_ref, w_ref, v_ref, x1_ref, y_ref, wd_scr):
    tm = e1_ref.shape[0]
    half = tm // 2
    pitch = _wd_pitch(tm)
    j = pl.program_id(1)

    @pl.when(j == 0)
    def _():
        eiota = lax.broadcasted_iota(I32, (PEER_KEYS, PEER_SLOTS), 0)

        def tile(t):
            e1 = e1_ref[pl.ds(t, 1), :]
            e2 = e2_ref[pl.ds(t, 1), :]
            wt = w_ref[pl.ds(t, 1), :]
            a = jnp.where(eiota == e1, wt, 0.0).astype(BF16)
            b = jnp.where(eiota == e2, 1.0, 0.0).astype(BF16)
            return _dot_nt(a, b)

        def token_pair(p, carry):
            packed = (_bf16_bits(tile(p)) >> 16) | _bf16_bits(tile(p + half))
            wd_scr[pl.ds(p, PEER_KEYS, stride=pitch), :] = packed
            return carry

        lax.fori_loop(0, half, token_pair, 0, unroll=16)
        y_ref[...] = x1_ref[...]

    n_e1 = EC_V // PEER_KEYS
    parts = []
    for c in range(n_e1):
        words = wd_scr[pl.ds(pl.multiple_of((j * n_e1 + c) * pitch, SUBLANES), half), :]
        lo = lax.bitcast_convert_type(words << 16, F32)
        hi = lax.bitcast_convert_type(words & jnp.uint32(0xFFFF0000), F32)
        parts.append(jnp.concatenate([lo, hi], axis=0))
    dense_w = jnp.concatenate(parts, axis=-1).astype(BF16)
    y_ref[...] += _dot(dense_w, v_ref[...])


def _peer_v(e1, e2, w, v_bf16, x1):
    tokens = x1.shape[0]
    tm = TM_V
    row = lambda i, j: (i, 0)
    return pl.pallas_call(
        _peer_v_body,
        out_shape=jax.ShapeDtypeStruct((tokens, D_MODEL), F32),
        grid=(tokens // tm, PEER_EXPERTS // EC_V),
        in_specs=[
            pl.BlockSpec((tm, PEER_SLOTS), row),
            pl.BlockSpec((tm, PEER_SLOTS), row),
            pl.BlockSpec((tm, PEER_SLOTS), row),
            pl.BlockSpec((EC_V, D_MODEL), lambda i, j: (j, 0)),
            pl.BlockSpec((tm, D_MODEL), row, pipeline_mode=pl.Buffered(1)),
        ],
        out_specs=pl.BlockSpec((tm, D_MODEL), row, pipeline_mode=pl.Buffered(1)),
        scratch_shapes=[pltpu.VMEM((PEER_KEYS * _wd_pitch(tm), PEER_KEYS), U32)],
        compiler_params=pltpu.CompilerParams(
            dimension_semantics=("arbitrary", "arbitrary"), vmem_limit_bytes=VMEM_LIMIT),
        name="peer_v",
    )(e1, e2, w, v_bf16, x1)


def _rope_tables(seq):
    inv = ROPE_THETA ** (-jnp.arange(0, ROPE_DIM, 2, dtype=F32) / ROPE_DIM)
    ang = jnp.arange(seq, dtype=F32)[:, None] * inv[None, :]
    cos = jnp.cos(ang)
    sin = jnp.sin(ang)
    lane = np.arange(LANES)
    in_head = lane % HEAD_DIM
    freq = in_head % ROPE_HALF
    cos_l = jnp.where(in_head < ROPE_DIM, cos[:, freq], 1.0)
    sin_l = jnp.where(in_head < ROPE_HALF, -sin[:, freq], jnp.where(in_head < ROPE_DIM, sin[:, freq], 0.0))
    return cos_l.astype(F32), sin_l.astype(F32)


def _layer(x, p):
    batch, seq, _ = x.shape
    tokens = batch * seq
    x2d = x.reshape(tokens, D_MODEL)
    cos_t, sin_t = _rope_tables(seq)
    qa, ka, va, qb, kb, vb, gates = _inproj(
        x2d, seq, p["norm1"], p["w_in"], p["gqa"], p["gka"], p["gqb"], p["gkb"], cos_t, sin_t, p["bd"])

    oa, = _banded_attention(
        qa.reshape(batch, seq, A_Q_W), ka.reshape(batch, seq, A_KV_W), va.reshape(batch, seq, A_KV_W),
        window=A_WINDOW, dil=1, q_block=lambda r: 0, kv_block=lambda r: 0, nkv=A_KV_HEADS, grp=A_GROUP,
        sink=p["sink"], name="attn_a")

    o_groups, lse_groups = [], []
    for gi, (win, dil) in enumerate(B_PATTERNS):
        sub = seq // dil
        blk = functools.partial(lambda r, gi: r * B_GROUPS + gi, gi=gi)
        o, lse = _banded_attention(
            qb.reshape(batch, sub, dil * B_W), kb.reshape(batch, sub, dil * B_W), vb.reshape(batch, sub, dil * B_W),
            window=win // (2 * dil), dil=dil, q_block=blk, kv_block=blk, nkv=B_HEADS_PER_GROUP, grp=1,
            want_lse=True, out_dtype=F32, name=f"attn_b{gi}")
        o_groups.append(o.reshape(tokens, B_GROUP_W))
        lse_groups.append(lse.reshape(tokens, B_GROUP_W))

    x1, h2, qp = _post(oa.reshape(tokens, A_Q_W), o_groups, lse_groups, gates, x2d,
                       p["w_o_a"], p["w_o_b"], p["w_out"], p["norm2"], p["w_query"])
    e1, e2, gate = _topk(qp, p["k1"], p["k2"])
    w = _peer_u(h2, p["u_t"], e1, e2, gate)
    y = _peer_v(e1, e2, w, p["v"], x1)
    return y.reshape(batch, seq, D_MODEL)


def _tile_gain(g):
    return jnp.tile(g.astype(F32), LANES // HEAD_DIM).reshape(1, LANES)


def kernel(x_prompt, x_sample, norm1, w_in, q_norm_a, k_norm_a, sink_a, q_norm_b, k_norm_b, w_o_a, w_o_b, w_out,
           norm2, w_query, sub_keys_1, sub_keys_2, expert_u, expert_v):
    depth = norm1.shape[0]
    head_of_lane = np.arange(PREP_COLS) // HEAD_DIM
    bd = jnp.asarray((head_of_lane[:, None] == head_of_lane[None, :]) / HEAD_DIM, dtype=BF16)
    params = []
    for l in range(depth):
        params.append({
            "norm1": norm1[l].reshape(1, D_MODEL),
            "w_in": w_in[l].astype(BF16),
            "gqa": _tile_gain(q_norm_a[l]), "gka": _tile_gain(k_norm_a[l]),
            "gqb": _tile_gain(q_norm_b[l]), "gkb": _tile_gain(k_norm_b[l]),
            "sink": sink_a[l].astype(F32),
            "w_o_a": w_o_a[l].astype(BF16), "w_o_b": w_o_b[l].astype(BF16), "w_out": w_out[l].astype(BF16),
            "norm2": norm2[l].reshape(1, D_MODEL),
            "w_query": w_query[l].astype(BF16),
            "k1": sub_keys_1[l].astype(BF16), "k2": sub_keys_2[l].astype(BF16),
            "u_t": expert_u[l].astype(BF16).T,
            "v": expert_v[l].astype(BF16),
            "bd": bd,
        })
    ys = []
    for x in (x_prompt, x_sample):
        for p in params:
            x = _layer(x, p)
        ys.append(x)
    return tuple(ys)
```

```python
import functools

import numpy as np
import jax
import jax.numpy as jnp
from jax import lax
from jax.experimental import pallas as pl
from jax.experimental.pallas import tpu as pltpu

F32 = jnp.float32
BF16 = jnp.bfloat16
I32 = jnp.int32
U32 = jnp.uint32

D_MODEL = 1024
HEAD_DIM = 64
ROPE_DIM = HEAD_DIM // 4
ROPE_HALF = ROPE_DIM // 2
ROPE_THETA = 500000.0
EPS = 1e-6
A_Q_HEADS = 8
A_KV_HEADS = 2
A_GROUP = A_Q_HEADS // A_KV_HEADS
A_WINDOW = 128
B_PATTERNS = ((128, 1), (512, 4), (2048, 16))
B_HEADS_PER_GROUP = 4
B_GROUPS = len(B_PATTERNS)
A_Q_W = A_Q_HEADS * HEAD_DIM
A_KV_W = A_KV_HEADS * HEAD_DIM
B_GROUP_W = B_HEADS_PER_GROUP * HEAD_DIM
B_W = B_GROUPS * B_GROUP_W
IN_COLS = A_Q_W + 2 * A_KV_W + 3 * B_W + 2 * D_MODEL
PEER_HEADS = 8
PEER_KEYS = 128
PEER_EXPERTS = PEER_KEYS * PEER_KEYS
PEER_HALF = 128
PEER_TOPK = 16
PEER_SLOTS = PEER_HEADS * PEER_TOPK
PEER_Q_W = PEER_HEADS * 2 * PEER_HALF

LANES = 128
SUBLANES = 8
MASK_VALUE = -1e30
NEG = float(np.finfo(np.float32).min)
VMEM_LIMIT = 56 * 1024 * 1024

OFF_QA = 0
OFF_KA = OFF_QA + A_Q_W
OFF_VA = OFF_KA + A_KV_W
OFF_QB = OFF_VA + A_KV_W
OFF_KB = OFF_QB + B_W
OFF_VB = OFF_KB + B_W
OFF_GA = OFF_VB + B_W
OFF_GB = OFF_GA + D_MODEL

TM_IN = 512
PREP_COLS = 256
TM_POST = 512
TM_TOPK = 128
TOPK_HEADS_PER_ITER = 4
TM_U = 2048
EC_U = 512
GATHER_ROWS = 8
TM_V = 1024
EC_V = 1024
ATTN_TQ = 256
ATTN_SUBQ = 128

def _dot(a, b):
    return jnp.dot(a, b, preferred_element_type=F32)


def _dot_nt(a, b):
    return lax.dot_general(a, b, (((1,), (1,)), ((), ())), preferred_element_type=F32)


def _rms(x, g):
    ms = jnp.mean(x * x, axis=-1, keepdims=True)
    return (x * lax.rsqrt(ms + EPS)) * g


def _head_prep(t, gain, cosv, sinv, bd, lane_lo):
    sq = t * t
    hi = sq.astype(BF16)
    lo = (sq - hi.astype(F32)).astype(BF16)
    ms = _dot(hi, bd) + _dot(lo, bd)
    tn = (t * lax.rsqrt(ms + EPS)) * gain
    out = []
    for c in range(t.shape[1] // LANES):
        tc = tn[:, c * LANES:(c + 1) * LANES]
        partner = jnp.where(lane_lo, pltpu.roll(tc, LANES - ROPE_HALF, 1), pltpu.roll(tc, ROPE_HALF, 1))
        out.append(tc * cosv + partner * sinv)
    return out


def _inproj_body(x_ref, n1_ref, w_ref, gqa_ref, gka_ref, gqb_ref, gkb_ref, cos_ref, sin_ref, bd_ref,
                 qa_ref, ka_ref, va_ref, *rest):
    qb_refs, kb_refs, vb_refs = rest[0:B_GROUPS], rest[B_GROUPS:2 * B_GROUPS], rest[2 * B_GROUPS:3 * B_GROUPS]
    gate_ref, scr = rest[3 * B_GROUPS], rest[3 * B_GROUPS + 1]
    tm = x_ref.shape[0]
    h = _rms(x_ref[...], n1_ref[...]).astype(BF16)
    cosv = cos_ref[...]
    sinv = sin_ref[...]
    bd = bd_ref[...]
    lane = lax.broadcasted_iota(I32, (1, LANES), 1)
    lane_lo = (lane % HEAD_DIM) < ROPE_HALF

    def prepped(off, width, gain_ref, scale, emit):
        step = min(width, PREP_COLS)
        gain = jnp.concatenate([gain_ref[...]] * (step // LANES), axis=-1)
        for c in range(width // step):
            t = _dot(h, w_ref[:, off + c * step: off + (c + 1) * step])
            chunks = _head_prep(t, gain, cosv, sinv, bd[:step, :step], lane_lo)
            emit(c, [tk * scale if scale != 1.0 else tk for tk in chunks])

    def flat(out_ref):
        def emit(c, chunks):
            for k, tk in enumerate(chunks):
                lo = c * PREP_COLS + k * LANES
                out_ref[:, lo:lo + LANES] = tk.astype(out_ref.dtype)
        return emit

    def by_residue(out_refs):
        def emit(g, chunks):
            dil = B_PATTERNS[g][1]
            for k, tk in enumerate(chunks):
                cols = slice(k * LANES, (k + 1) * LANES)
                if dil == 1:
                    out_refs[g][0, :, cols] = tk.astype(out_refs[g].dtype)
                    continue
                scr[k] = tk
                for r in range(dil):
                    rows = scr[k, pl.ds(r, tm // dil, stride=dil), :]
                    out_refs[g][r, :, cols] = rows.astype(out_refs[g].dtype)
        return emit

    assert PREP_COLS == B_GROUP_W
    qscale = HEAD_DIM ** -0.5
    prepped(OFF_QA, A_Q_W, gqa_ref, qscale, flat(qa_ref))
    prepped(OFF_KA, A_KV_W, gka_ref, 1.0, flat(ka_ref))
    va_ref[...] = _dot(h, w_ref[:, OFF_VA:OFF_VA + A_KV_W]).astype(va_ref.dtype)
    prepped(OFF_QB, B_W, gqb_ref, qscale, by_residue(qb_refs))
    prepped(OFF_KB, B_W, gkb_ref, 1.0, by_residue(kb_refs))
    for g in range(B_GROUPS):
        t = _dot(h, w_ref[:, OFF_VB + g * B_GROUP_W: OFF_VB + (g + 1) * B_GROUP_W])
        by_residue(vb_refs)(g, [t[:, k * LANES:(k + 1) * LANES] for k in range(B_GROUP_W // LANES)])
    for c in range(2 * D_MODEL // 512):
        g = _dot(h, w_ref[:, OFF_GA + c * 512: OFF_GA + (c + 1) * 512])
        gate_ref[:, c * 512:(c + 1) * 512] = jax.nn.sigmoid(g)


def _inproj(x2d, seq, n1, w_in, gqa, gka, gqb, gkb, cos_t, sin_t, bd):
    tokens = x2d.shape[0]
    tm = TM_IN
    blocks_per_seq = seq // tm
    row = lambda i: (i, 0)
    const = lambda i: (0, 0)
    pos = lambda i: (i % blocks_per_seq, 0)
    batch = tokens // seq
    group_shapes = [jax.ShapeDtypeStruct((batch, dil, seq // dil, B_GROUP_W), BF16) for _, dil in B_PATTERNS]
    group_specs = [pl.BlockSpec((None, dil, tm // dil, B_GROUP_W),
                                lambda i: (i // blocks_per_seq, 0, i % blocks_per_seq, 0)) for _, dil in B_PATTERNS]
    out_shape = (
        jax.ShapeDtypeStruct((tokens, A_Q_W), BF16),
        jax.ShapeDtypeStruct((tokens, A_KV_W), BF16),
        jax.ShapeDtypeStruct((tokens, A_KV_W), BF16),
        *group_shapes, *group_shapes, *group_shapes,
        jax.ShapeDtypeStruct((tokens, 2 * D_MODEL), F32),
    )
    return pl.pallas_call(
        _inproj_body,
        out_shape=out_shape,
        scratch_shapes=[pltpu.VMEM((B_GROUP_W // LANES, tm, LANES), F32)],
        grid=(tokens // tm,),
        in_specs=[
            pl.BlockSpec((tm, D_MODEL), row),
            pl.BlockSpec((1, D_MODEL), const),
            pl.BlockSpec((D_MODEL, IN_COLS), const),
            pl.BlockSpec((1, LANES), const),
            pl.BlockSpec((1, LANES), const),
            pl.BlockSpec((1, LANES), const),
            pl.BlockSpec((1, LANES), const),
            pl.BlockSpec((tm, LANES), pos),
            pl.BlockSpec((tm, LANES), pos),
            pl.BlockSpec((PREP_COLS, PREP_COLS), const),
        ],
        out_specs=(
            pl.BlockSpec((tm, A_Q_W), row),
            pl.BlockSpec((tm, A_KV_W), row),
            pl.BlockSpec((tm, A_KV_W), row),
            *group_specs, *group_specs, *group_specs,
            pl.BlockSpec((tm, 2 * D_MODEL), row),
        ),
        compiler_params=pltpu.CompilerParams(
            dimension_semantics=("arbitrary",), vmem_limit_bytes=VMEM_LIMIT),
        name="inproj",
    )(x2d, n1, w_in, gqa, gka, gqb, gkb, cos_t, sin_t, bd)


def _attn_body(*refs, window, tq, kw, length, nkv, grp, has_sink, want_lse):
    refs = list(refs)
    sink_ref = refs.pop(0) if has_sink else None
    q_ref, k_ref, v_ref, o_ref = refs[:4]
    lse_ref = refs[4] if want_lse else None

    q0 = pl.program_id(1) * tq
    if kw == length:
        start = 0
        kwin = k_ref[...]
        vwin = v_ref[...]
    else:
        start = pl.multiple_of(jnp.clip(q0 - window, 0, length - kw), window)
        kwin = k_ref[pl.ds(start, kw), :]
        vwin = v_ref[pl.ds(start, kw), :]
    sq = min(tq, ATTN_SUBQ) if grp > 1 else tq
    qi = lax.broadcasted_iota(I32, (sq, kw), 0)
    kj = lax.broadcasted_iota(I32, (sq, kw), 1)
    rel = kj - qi + (start - q0)
    valid = [jnp.abs(rel - r0) <= window for r0 in range(0, tq, sq)]

    o_parts = [None] * (nkv * grp)
    lse_parts = [None] * (nkv * grp)
    for j in range(nkv):
        kh = kwin[:, j * HEAD_DIM:(j + 1) * HEAD_DIM]
        vh = vwin[:, j * HEAD_DIM:(j + 1) * HEAD_DIM]
        for g in range(grp):
            h = j * grp + g
            o_rows, lse_rows = [], []
            for i, r0 in enumerate(range(0, tq, sq)):
                qh = q_ref[r0:r0 + sq, h * HEAD_DIM:(h + 1) * HEAD_DIM]
                s = jnp.where(valid[i], _dot_nt(qh, kh), MASK_VALUE)
                m = jnp.max(s, axis=-1, keepdims=True)
                if has_sink:
                    m = jnp.maximum(m, sink_ref[h])
                p = jnp.exp(s - m)
                denom = jnp.sum(p, axis=-1, keepdims=True)
                if has_sink:
                    denom = denom + jnp.exp(sink_ref[h] - m)
                o_rows.append(_dot(p.astype(BF16), vh) / denom)
                if want_lse:
                    lse_rows.append(jnp.broadcast_to(m + jnp.log(denom), (sq, HEAD_DIM)))
            o_parts[h] = o_rows[0] if len(o_rows) == 1 else jnp.concatenate(o_rows, axis=0)
            if want_lse:
                lse_parts[h] = lse_rows[0] if len(lse_rows) == 1 else jnp.concatenate(lse_rows, axis=0)
    o_ref[...] = jnp.concatenate(o_parts, axis=-1).astype(o_ref.dtype)
    if want_lse:
        lse_ref[...] = jnp.concatenate(lse_parts, axis=-1)


def _banded_attention(q, k, v, *, window, nkv, grp, sink=None, want_lse=False, out_dtype=BF16, name):
    nseq, length, wq = q.shape
    wk = nkv * HEAD_DIM
    tq = min(ATTN_TQ, length)
    kw = min(length, tq + 2 * window)
    body = functools.partial(_attn_body, window=window, tq=tq, kw=kw, length=length, nkv=nkv, grp=grp,
                             has_sink=sink is not None, want_lse=want_lse)
    in_specs = [
        pl.BlockSpec((None, tq, wq), lambda b, t: (b, t, 0)),
        pl.BlockSpec((None, length, wk), lambda b, t: (b, 0, 0)),
        pl.BlockSpec((None, length, wk), lambda b, t: (b, 0, 0)),
    ]
    args = [q, k, v]
    if sink is not None:
        in_specs.insert(0, pl.BlockSpec(memory_space=pltpu.SMEM))
        args.insert(0, sink)
    o_spec = pl.BlockSpec((None, tq, wq), lambda b, t: (b, t, 0))
    out_shape = [jax.ShapeDtypeStruct((nseq, length, wq), out_dtype)]
    out_specs = [o_spec]
    if want_lse:
        out_shape.append(jax.ShapeDtypeStruct((nseq, length, wq), F32))
        out_specs.append(o_spec)
    return pl.pallas_call(
        body,
        out_shape=tuple(out_shape),
        grid=(nseq, length // tq),
        in_specs=in_specs,
        out_specs=tuple(out_specs),
        compiler_params=pltpu.CompilerParams(
            dimension_semantics=("arbitrary", "arbitrary"), vmem_limit_bytes=VMEM_LIMIT),
        name=name,
    )(*args)


def _post_body(oa_ref, o1_ref, o2_ref, o3_ref, l1_ref, l2_ref, l3_ref, ga_ref, gb_ref, x_ref,
               woa_ref, wob_ref, wout_ref, n2_ref, wq_ref,
               x1_ref, h2_ref, qp_ref, scr):
    tm = x_ref.shape[0]

    def tokens_major(ref, g):
        dil = B_PATTERNS[g][1]
        if dil == 1:
            return ref[0]
        halves = []
        for k in range(B_GROUP_W // LANES):
            for r in range(dil):
                scr[k, pl.ds(r, tm // dil, stride=dil), :] = ref[r, :, k * LANES:(k + 1) * LANES]
            halves.append(scr[k])
        return jnp.concatenate(halves, axis=-1)

    l1, l2, l3 = [tokens_major(ref, g) for g, ref in enumerate((l1_ref, l2_ref, l3_ref))]
    mx = jnp.maximum(jnp.maximum(l1, l2), l3)
    e1, e2, e3 = jnp.exp(l1 - mx), jnp.exp(l2 - mx), jnp.exp(l3 - mx)
    tot = e1 + e2 + e3
    o1, o2, o3 = [tokens_major(ref, g) for g, ref in enumerate((o1_ref, o2_ref, o3_ref))]
    ob = (e1 / tot) * o1 + (e2 / tot) * o2 + (e3 / tot) * o3
    ya = _dot(oa_ref[...], woa_ref[...])
    yb = _dot(ob.astype(BF16), wob_ref[...])
    merged = ga_ref[...] * ya + gb_ref[...] * yb
    x1 = x_ref[...] + _dot(merged.astype(BF16), wout_ref[...])
    x1_ref[...] = x1
    h2 = _rms(x1, n2_ref[...]).astype(BF16)
    h2_ref[...] = h2
    for c in range(PEER_Q_W // PEER_HALF):
        qp_ref[c] = _dot(h2, wq_ref[:, c * PEER_HALF:(c + 1) * PEER_HALF]).astype(qp_ref.dtype)


def _post(oa, o_groups, lse_groups, gates, x2d, seq, w_o_a, w_o_b, w_out, n2, w_query):
    tokens = x2d.shape[0]
    tm = TM_POST
    blocks_per_seq = seq // tm
    row = lambda i: (i, 0)
    const = lambda i: (0, 0)
    nq = PEER_Q_W // PEER_HALF
    group_specs = [pl.BlockSpec((None, dil, tm // dil, B_GROUP_W),
                                lambda i: (i // blocks_per_seq, 0, i % blocks_per_seq, 0)) for _, dil in B_PATTERNS]
    return pl.pallas_call(
        _post_body,
        out_shape=(
            jax.ShapeDtypeStruct((tokens, D_MODEL), F32),
            jax.ShapeDtypeStruct((tokens, D_MODEL), BF16),
            jax.ShapeDtypeStruct((nq, tokens, PEER_HALF), BF16),
        ),
        grid=(tokens // tm,),
        scratch_shapes=[pltpu.VMEM((B_GROUP_W // LANES, tm, LANES), F32)],
        in_specs=[pl.BlockSpec((tm, A_Q_W), row)] + group_specs + group_specs
        + [
            pl.BlockSpec((tm, D_MODEL), lambda i: (i, 0)),
            pl.BlockSpec((tm, D_MODEL), lambda i: (i, 1)),
            pl.BlockSpec((tm, D_MODEL), row),
            pl.BlockSpec((A_Q_W, D_MODEL), const),
            pl.BlockSpec((B_GROUP_W, D_MODEL), const),
            pl.BlockSpec((D_MODEL, D_MODEL), const),
            pl.BlockSpec((1, D_MODEL), const),
            pl.BlockSpec((D_MODEL, PEER_Q_W), const),
        ],
        out_specs=(
            pl.BlockSpec((tm, D_MODEL), row),
            pl.BlockSpec((tm, D_MODEL), row),
            pl.BlockSpec((nq, tm, PEER_HALF), lambda i: (0, i, 0)),
        ),
        compiler_params=pltpu.CompilerParams(
            dimension_semantics=("arbitrary",), vmem_limit_bytes=VMEM_LIMIT),
        name="post",
    )(oa, *o_groups, *lse_groups, gates, gates, x2d, w_o_a, w_o_b, w_out, n2, w_query)


def _tree(xs, op):
    xs = list(xs)
    while len(xs) > 1:
        nxt = [op(xs[i], xs[i + 1]) for i in range(0, len(xs) - 1, 2)]
        if len(xs) % 2:
            nxt.append(xs[-1])
        xs = nxt
    return xs[0]


def _all_rows(blocks, op):
    r = _tree(blocks, op)
    for shift in (4, 2, 1):
        r = op(r, pltpu.roll(r, shift, 0))
    return r


def _pack(rows, sub):
    blk = rows[0]
    for i in range(1, SUBLANES):
        blk = jnp.where(sub == i, rows[i], blk)
    return blk


def _select_top(blocks, ids, payload, k):
    past = float(np.finfo(np.float32).max)
    out = []
    for _ in range(k):
        m = _all_rows(blocks, jnp.maximum)
        pos = _all_rows([jnp.where(b == m, i, past) for b, i in zip(blocks, ids)], jnp.minimum)
        sel = [i == pos for i in ids]
        picked = None
        if payload is not None:
            picked = _all_rows([jnp.where(s, p, -1.0) for s, p in zip(sel, payload)], jnp.maximum)
        blocks = [jnp.where(s, NEG, b) for s, b in zip(sel, blocks)]
        out.append((m, pos, picked))
    return out


def _candidate_rows(first, second, combine, sub):
    half = SUBLANES
    first_hi = pltpu.roll(_pack(first[half:], sub), 4, 0)
    lo = _pack(second[:half], sub)
    hi = _pack(second[half:], sub)
    w = jnp.where
    a_blocks = [first[0], first[0], first[1],
                w(sub < 5, first[2], first[5]),
                w(sub < 4, first[3], first[4]),
                w(sub < 2, first[6], w(sub < 4, first[7], first_hi)),
                first_hi]
    b_blocks = [lo, hi, lo,
                w(sub < 5, lo, pltpu.roll(lo, 5, 0)),
                w(sub < 4, lo, pltpu.roll(lo, 4, 0)),
                w(sub < 2, lo, w(sub < 4, pltpu.roll(lo, 2, 0), second[0])),
                second[0]]
    t = PEER_TOPK
    flat = [sub, half + sub, t + sub,
            w(sub < 5, 2 * t + sub, 5 * t + sub - 5),
            w(sub < 4, 3 * t + sub, 4 * t + sub - 4),
            w(sub < 2, 6 * t + sub, w(sub < 4, 7 * t + sub - 2, (half + sub - 4) * t)),
            (half + 4 + sub) * t]
    used = [None, None, None, sub < 7, sub < 7, None, sub < 4]
    flat = [f if u is None else w(u, f, -1) for f, u in zip(flat, used)]
    return [combine(a, b) for a, b in zip(a_blocks, b_blocks)], flat


def _head_experts(s1, s2):
    tm = s1.shape[1]
    sub = lax.broadcasted_iota(I32, (SUBLANES, tm), 0)
    subf = sub.astype(F32)
    key_ids = [subf + float(SUBLANES * g) for g in range(PEER_KEYS // SUBLANES)]

    def half(s):
        top = _select_top([s[SUBLANES * g:SUBLANES * (g + 1)] for g in range(PEER_KEYS // SUBLANES)], key_ids,
                          None, PEER_TOPK)
        return [t[0] for t in top], [t[1] for t in top]

    v1, i1 = half(s1)
    v2, i2 = half(s2)
    cand, flat = _candidate_rows(v1, v2, lambda a, b: a + b, sub)
    cand = [jnp.where(f >= 0, c, NEG) for c, f in zip(cand, flat)]
    expert, _ = _candidate_rows(i1, i2, lambda a, b: a * float(PEER_KEYS) + b, sub)
    unused_base = PEER_TOPK * PEER_TOPK
    ids = [jnp.where(f >= 0, f, unused_base + SUBLANES * g + sub).astype(F32) for g, f in enumerate(flat)]
    top = _select_top(cand, ids, expert, PEER_TOPK)
    ex = [jnp.exp(_pack([t[0] for t in top[o:o + SUBLANES]], sub) - top[0][0]) for o in (0, SUBLANES)]
    tot = _all_rows(ex, jnp.add)
    out = []
    for o, e in zip((0, SUBLANES), ex):
        code = _pack([t[2] for t in top[o:o + SUBLANES]], sub)
        e1 = jnp.floor(code * (1.0 / PEER_KEYS))
        out.append((e / tot, e1, code - e1 * float(PEER_KEYS)))
    return out


def _peer_head(h, qp_ref, k1_ref, k2_ref, e1t_scr, e2t_scr, gt_scr):
    s1 = _dot_nt(k1_ref[h], qp_ref[2 * h])
    s2 = _dot_nt(k2_ref[h], qp_ref[2 * h + 1])
    base = pl.multiple_of(h * PEER_TOPK, PEER_TOPK)
    for o, (gate, e1, e2) in zip((0, SUBLANES), _head_experts(s1, s2)):
        rows = pl.ds(pl.multiple_of(base + o, SUBLANES), SUBLANES)
        gt_scr[rows, :] = gate
        e1t_scr[rows, :] = e1
        e2t_scr[rows, :] = e2


def _topk_body(qp_ref, k1_ref, k2_ref, e1_ref, e2_ref, g_ref, e1t_scr, e2t_scr, gt_scr):
    def head_group(i, carry):
        for o in range(TOPK_HEADS_PER_ITER):
            _peer_head(TOPK_HEADS_PER_ITER * i + o, qp_ref, k1_ref, k2_ref, e1t_scr, e2t_scr, gt_scr)
        return carry

    lax.fori_loop(0, PEER_HEADS // TOPK_HEADS_PER_ITER, head_group, 0)
    e1_ref[...] = e1t_scr[...].T.astype(I32)
    e2_ref[...] = e2t_scr[...].T.astype(I32)
    g_ref[...] = gt_scr[...].T


def _topk(qp, k1, k2):
    nq, tokens, _ = qp.shape
    tm = TM_TOPK
    assert tm == PEER_SLOTS
    row = lambda i: (i, 0)
    const3 = lambda i: (0, 0, 0)
    return pl.pallas_call(
        _topk_body,
        out_shape=(
            jax.ShapeDtypeStruct((tokens, PEER_SLOTS), I32),
            jax.ShapeDtypeStruct((tokens, PEER_SLOTS), I32),
            jax.ShapeDtypeStruct((tokens, PEER_SLOTS), F32),
        ),
        grid=(tokens // tm,),
        in_specs=[
            pl.BlockSpec((nq, tm, PEER_HALF), lambda i: (0, i, 0)),
            pl.BlockSpec((PEER_HEADS, PEER_KEYS, PEER_HALF), const3),
            pl.BlockSpec((PEER_HEADS, PEER_KEYS, PEER_HALF), const3),
        ],
        out_specs=(
            pl.BlockSpec((tm, PEER_SLOTS), row),
            pl.BlockSpec((tm, PEER_SLOTS), row),
            pl.BlockSpec((tm, PEER_SLOTS), row),
        ),
        scratch_shapes=[
            pltpu.VMEM((PEER_SLOTS, tm), F32),
            pltpu.VMEM((PEER_SLOTS, tm), F32),
            pltpu.VMEM((PEER_SLOTS, tm), F32),
        ],
        compiler_params=pltpu.CompilerParams(
            dimension_semantics=("arbitrary",), vmem_limit_bytes=VMEM_LIMIT),
        name="topk",
    )(qp, k1, k2)


def _peer_u_body(h2_ref, ut_ref, e1_ref, e2_ref, g_ref, w_ref, acc_ref, sc_ref):
    j = pl.program_id(1)
    n_e1 = EC_U // PEER_KEYS

    @pl.when(j == 0)
    def _():
        acc_ref[...] = jnp.zeros_like(acc_ref)
        sc_ref[...] = jnp.zeros_like(sc_ref)

    prev = sc_ref.at[(j + 1) % 2]
    for r in range(0, h2_ref.shape[0], GATHER_ROWS):
        rows = slice(r, r + GATHER_ROWS)
        e1 = e1_ref[rows, :]
        e2 = e2_ref[rows, :]
        acc = acc_ref[rows, :]
        for c in range(n_e1):
            picked = jnp.take_along_axis(prev[rows, c * PEER_KEYS:(c + 1) * PEER_KEYS], e2, axis=1)
            acc = jnp.where(e1 == (j - 1) * n_e1 + c, picked, acc)
        acc_ref[rows, :] = acc
    sc_ref[j % 2] = _dot(h2_ref[...], ut_ref[...])

    @pl.when(j == pl.num_programs(1) - 1)
    def _():
        acc = acc_ref[...]
        w_ref[...] = g_ref[...] * (0.5 * acc * (1.0 + lax.erf(acc * np.float32(np.sqrt(0.5)))))


def _peer_u(h2, u_t, e1, e2, gate):
    tokens = h2.shape[0]
    tm = TM_U
    n_chunks = PEER_EXPERTS // EC_U
    row = lambda i, j: (i, 0)
    return pl.pallas_call(
        _peer_u_body,
        out_shape=jax.ShapeDtypeStruct((tokens, PEER_SLOTS), F32),
        grid=(tokens // tm, n_chunks + 1),
        in_specs=[
            pl.BlockSpec((tm, D_MODEL), row),
            pl.BlockSpec((D_MODEL, EC_U), lambda i, j: (0, jnp.minimum(j, n_chunks - 1))),
            pl.BlockSpec((tm, PEER_SLOTS), row),
            pl.BlockSpec((tm, PEER_SLOTS), row),
            pl.BlockSpec((tm, PEER_SLOTS), row),
        ],
        out_specs=pl.BlockSpec((tm, PEER_SLOTS), row),
        scratch_shapes=[pltpu.VMEM((tm, PEER_SLOTS), F32), pltpu.VMEM((2, tm, EC_U), F32)],
        compiler_params=pltpu.CompilerParams(
            dimension_semantics=("arbitrary", "arbitrary"), vmem_limit_bytes=VMEM_LIMIT),
        name="peer_u",
    )(h2, u_t, e1, e2, gate)


def _wd_pitch(tm):
    tiles = tm // 2 // SUBLANES
    return (tiles + 1 - tiles % 2) * SUBLANES


def _bf16_bits(x):
    return lax.bitcast_convert_type(x.astype(BF16).astype(F32), U32)


def _peer_v_body(e1_ref, e2_ref, w_ref, v_ref, x1_ref, y_ref, wd_scr):
    tm = e1_ref.shape[0]
    half = tm // 2
    pitch = _wd_pitch(tm)
    j = pl.program_id(1)

    @pl.when(j == 0)
    def _():
        eiota = lax.broadcasted_iota(I32, (PEER_KEYS, PEER_SLOTS), 0)

        def tile(t):
            e1 = e1_ref[pl.ds(t, 1), :]
            e2 = e2_ref[pl.ds(t, 1), :]
            wt = w_ref[pl.ds(t, 1), :]
            a = jnp.where(eiota == e1, wt, 0.0).astype(BF16)
            b = jnp.where(eiota == e2, 1.0, 0.0).astype(BF16)
            return _dot_nt(a, b)

        def token_pair(p, carry):
            packed = (_bf16_bits(tile(p)) >> 16) | _bf16_bits(tile(p + half))
            wd_scr[pl.ds(p, PEER_KEYS, stride=pitch), :] = packed
            return carry

        lax.fori_loop(0, half, token_pair, 0, unroll=16)
        y_ref[...] = x1_ref[...]

    n_e1 = EC_V // PEER_KEYS
    parts = []
    for c in range(n_e1):
        words = wd_scr[pl.ds(pl.multiple_of((j * n_e1 + c) * pitch, SUBLANES), half), :]
        lo = lax.bitcast_convert_type(words << 16, F32)
        hi = lax.bitcast_convert_type(words & jnp.uint32(0xFFFF0000), F32)
        parts.append(jnp.concatenate([lo, hi], axis=0))
    dense_w = jnp.concatenate(parts, axis=-1).astype(BF16)
    y_ref[...] += _dot(dense_w, v_ref[...])


def _peer_v(e1, e2, w, v_bf16, x1):
    tokens = x1.shape[0]
    tm = TM_V
    row = lambda i, j: (i, 0)
    return pl.pallas_call(
        _peer_v_body,
        out_shape=jax.ShapeDtypeStruct((tokens, D_MODEL), F32),
        grid=(tokens // tm, PEER_EXPERTS // EC_V),
        in_specs=[
            pl.BlockSpec((tm, PEER_SLOTS), row),
            pl.BlockSpec((tm, PEER_SLOTS), row),
            pl.BlockSpec((tm, PEER_SLOTS), row),
            pl.BlockSpec((EC_V, D_MODEL), lambda i, j: (j, 0)),
            pl.BlockSpec((tm, D_MODEL), row, pipeline_mode=pl.Buffered(1)),
        ],
        out_specs=pl.BlockSpec((tm, D_MODEL), row, pipeline_mode=pl.Buffered(1)),
        scratch_shapes=[pltpu.VMEM((PEER_KEYS * _wd_pitch(tm), PEER_KEYS), U32)],
        compiler_params=pltpu.CompilerParams(
            dimension_semantics=("arbitrary", "arbitrary"), vmem_limit_bytes=VMEM_LIMIT),
        name="peer_v",
    )(e1, e2, w, v_bf16, x1)


def _rope_tables(seq):
    inv = ROPE_THETA ** (-jnp.arange(0, ROPE_DIM, 2, dtype=F32) / ROPE_DIM)
    ang = jnp.arange(seq, dtype=F32)[:, None] * inv[None, :]
    cos = jnp.cos(ang)
    sin = jnp.sin(ang)
    lane = np.arange(LANES)
    in_head = lane % HEAD_DIM
    freq = in_head % ROPE_HALF
    cos_l = jnp.where(in_head < ROPE_DIM, cos[:, freq], 1.0)
    sin_l = jnp.where(in_head < ROPE_HALF, -sin[:, freq], jnp.where(in_head < ROPE_DIM, sin[:, freq], 0.0))
    return cos_l.astype(F32), sin_l.astype(F32)


def _layer(x, p):
    batch, seq, _ = x.shape
    tokens = batch * seq
    x2d = x.reshape(tokens, D_MODEL)
    cos_t, sin_t = _rope_tables(seq)
    outs = _inproj(x2d, seq, p["norm1"], p["w_in"], p["gqa"], p["gka"], p["gqb"], p["gkb"], cos_t, sin_t, p["bd"])
    qa, ka, va = outs[:3]
    qb, kb, vb = outs[3:3 + B_GROUPS], outs[3 + B_GROUPS:3 + 2 * B_GROUPS], outs[3 + 2 * B_GROUPS:3 + 3 * B_GROUPS]
    gates = outs[3 + 3 * B_GROUPS]

    oa, = _banded_attention(
        qa.reshape(batch, seq, A_Q_W), ka.reshape(batch, seq, A_KV_W), va.reshape(batch, seq, A_KV_W),
        window=A_WINDOW, nkv=A_KV_HEADS, grp=A_GROUP, sink=p["sink"], name="attn_a")

    o_groups, lse_groups = [], []
    for gi, (win, dil) in enumerate(B_PATTERNS):
        sub = seq // dil
        as_seqs = lambda t: t.reshape(batch * dil, sub, B_GROUP_W)
        o, lse = _banded_attention(
            as_seqs(qb[gi]), as_seqs(kb[gi]), as_seqs(vb[gi]), window=win // (2 * dil),
            nkv=B_HEADS_PER_GROUP, grp=1, want_lse=True, out_dtype=F32, name=f"attn_b{gi}")
        o_groups.append(o.reshape(batch, dil, sub, B_GROUP_W))
        lse_groups.append(lse.reshape(batch, dil, sub, B_GROUP_W))

    x1, h2, qp = _post(oa.reshape(tokens, A_Q_W), o_groups, lse_groups, gates, x2d, seq,
                       p["w_o_a"], p["w_o_b"], p["w_out"], p["norm2"], p["w_query"])
    e1, e2, gate = _topk(qp, p["k1"], p["k2"])
    w = _peer_u(h2, p["u_t"], e1, e2, gate)
    y = _peer_v(e1, e2, w, p["v"], x1)
    return y.reshape(batch, seq, D_MODEL)


def _tile_gain(g):
    return jnp.tile(g.astype(F32), LANES // HEAD_DIM).reshape(1, LANES)


def kernel(x_prompt, x_sample, norm1, w_in, q_norm_a, k_norm_a, sink_a, q_norm_b, k_norm_b, w_o_a, w_o_b, w_out,
           norm2, w_query, sub_keys_1, sub_keys_2, expert_u, expert_v):
    depth = norm1.shape[0]
    head_of_lane = np.arange(PREP_COLS) // HEAD_DIM
    bd = jnp.asarray((head_of_lane[:, None] == head_of_lane[None, :]) / HEAD_DIM, dtype=BF16)
    params = []
    for l in range(depth):
        params.append({
            "norm1": norm1[l].reshape(1, D_MODEL),
            "w_in": w_in[l].astype(BF16),
            "gqa": _tile_gain(q_norm_a[l]), "gka": _tile_gain(k_norm_a[l]),
            "gqb": _tile_gain(q_norm_b[l]), "gkb": _tile_gain(k_norm_b[l]),
            "sink": sink_a[l].astype(F32),
            "w_o_a": w_o_a[l].astype(BF16), "w_o_b": w_o_b[l].astype(BF16), "w_out": w_out[l].astype(BF16),
            "norm2": norm2[l].reshape(1, D_MODEL),
            "w_query": w_query[l].astype(BF16),
            "k1": sub_keys_1[l].astype(BF16), "k2": sub_keys_2[l].astype(BF16),
            "u_t": expert_u[l].astype(BF16).T,
            "v": expert_v[l].astype(BF16),
            "bd": bd,
        })
    ys = []
    for x in (x_prompt, x_sample):
        for p in params:
            x = _layer(x, p)
        ys.append(x)
    return tuple(ys)
```

```python
import functools

import numpy as np
import jax
import jax.numpy as jnp
from jax import lax
from jax.experimental import pallas as pl
from jax.experimental.pallas import tpu as pltpu

F32 = jnp.float32
BF16 = jnp.bfloat16
I32 = jnp.int32
U32 = jnp.uint32

D_MODEL = 1024
HEAD_DIM = 64
ROPE_DIM = HEAD_DIM // 4
ROPE_HALF = ROPE_DIM // 2
ROPE_THETA = 500000.0
EPS = 1e-6
A_Q_HEADS = 8
A_KV_HEADS = 2
A_GROUP = A_Q_HEADS // A_KV_HEADS
A_WINDOW = 128
B_PATTERNS = ((128, 1), (512, 4), (2048, 16))
B_HEADS_PER_GROUP = 4
B_GROUPS = len(B_PATTERNS)
A_Q_W = A_Q_HEADS * HEAD_DIM
A_KV_W = A_KV_HEADS * HEAD_DIM
B_GROUP_W = B_HEADS_PER_GROUP * HEAD_DIM
B_W = B_GROUPS * B_GROUP_W
IN_COLS = A_Q_W + 2 * A_KV_W + 3 * B_W + 2 * D_MODEL
PEER_HEADS = 8
PEER_KEYS = 128
PEER_EXPERTS = PEER_KEYS * PEER_KEYS
PEER_HALF = 128
PEER_TOPK = 16
PEER_SLOTS = PEER_HEADS * PEER_TOPK
PEER_Q_W = PEER_HEADS * 2 * PEER_HALF

LANES = 128
SUBLANES = 8
MASK_VALUE = -1e30
NEG = float(np.finfo(np.float32).min)
VMEM_LIMIT = 56 * 1024 * 1024

OFF_QA = 0
OFF_KA = OFF_QA + A_Q_W
OFF_VA = OFF_KA + A_KV_W
OFF_QB = OFF_VA + A_KV_W
OFF_KB = OFF_QB + B_W
OFF_VB = OFF_KB + B_W
OFF_GA = OFF_VB + B_W
OFF_GB = OFF_GA + D_MODEL

TM_IN = 512
PREP_COLS = 256
TM_POST = 512
TM_TOPK = 128
TOPK_HEADS_PER_ITER = 8
TM_U = 2048
EC_U = 512
GATHER_ROWS = 8
TM_V = 1024
EC_V = 1024
ATTN_TQ = 256
ATTN_SUBQ = 128

def _dot(a, b):
    return jnp.dot(a, b, preferred_element_type=F32)


def _dot_nt(a, b):
    return lax.dot_general(a, b, (((1,), (1,)), ((), ())), preferred_element_type=F32)


def _rms(x, g):
    ms = jnp.mean(x * x, axis=-1, keepdims=True)
    return (x * lax.rsqrt(ms + EPS)) * g


def _head_prep(t, gain, cosv, sinv, bd, lane_lo):
    sq = t * t
    hi = sq.astype(BF16)
    lo = (sq - hi.astype(F32)).astype(BF16)
    ms = _dot(hi, bd) + _dot(lo, bd)
    tn = (t * lax.rsqrt(ms + EPS)) * gain
    out = []
    for c in range(t.shape[1] // LANES):
        tc = tn[:, c * LANES:(c + 1) * LANES]
        partner = jnp.where(lane_lo, pltpu.roll(tc, LANES - ROPE_HALF, 1), pltpu.roll(tc, ROPE_HALF, 1))
        out.append(tc * cosv + partner * sinv)
    return out


def _inproj_body(x_ref, n1_ref, w_ref, gqa_ref, gka_ref, gqb_ref, gkb_ref, cos_ref, sin_ref, bd_ref,
                 qa_ref, ka_ref, va_ref, *rest):
    qb_refs, kb_refs, vb_refs = rest[0:B_GROUPS], rest[B_GROUPS:2 * B_GROUPS], rest[2 * B_GROUPS:3 * B_GROUPS]
    gate_ref, scr = rest[3 * B_GROUPS], rest[3 * B_GROUPS + 1]
    tm = x_ref.shape[0]
    h = _rms(x_ref[...], n1_ref[...]).astype(BF16)
    cosv = cos_ref[...]
    sinv = sin_ref[...]
    bd = bd_ref[...]
    lane = lax.broadcasted_iota(I32, (1, LANES), 1)
    lane_lo = (lane % HEAD_DIM) < ROPE_HALF

    def prepped(off, width, gain_ref, scale, emit):
        step = min(width, PREP_COLS)
        gain = jnp.concatenate([gain_ref[...]] * (step // LANES), axis=-1)
        for c in range(width // step):
            t = _dot(h, w_ref[:, off + c * step: off + (c + 1) * step])
            chunks = _head_prep(t, gain, cosv, sinv, bd[:step, :step], lane_lo)
            emit(c, [tk * scale if scale != 1.0 else tk for tk in chunks])

    def flat(out_ref):
        def emit(c, chunks):
            for k, tk in enumerate(chunks):
                lo = c * PREP_COLS + k * LANES
                out_ref[:, lo:lo + LANES] = tk.astype(out_ref.dtype)
        return emit

    def by_residue(out_refs):
        def emit(g, chunks):
            dil = B_PATTERNS[g][1]
            for k, tk in enumerate(chunks):
                cols = slice(k * LANES, (k + 1) * LANES)
                if dil == 1:
                    out_refs[g][0, :, cols] = tk.astype(out_refs[g].dtype)
                    continue
                scr[k] = tk
                for r in range(dil):
                    rows = scr[k, pl.ds(r, tm // dil, stride=dil), :]
                    out_refs[g][r, :, cols] = rows.astype(out_refs[g].dtype)
        return emit

    assert PREP_COLS == B_GROUP_W
    qscale = HEAD_DIM ** -0.5
    prepped(OFF_QA, A_Q_W, gqa_ref, qscale, flat(qa_ref))
    prepped(OFF_KA, A_KV_W, gka_ref, 1.0, flat(ka_ref))
    va_ref[...] = _dot(h, w_ref[:, OFF_VA:OFF_VA + A_KV_W]).astype(va_ref.dtype)
    prepped(OFF_QB, B_W, gqb_ref, qscale, by_residue(qb_refs))
    prepped(OFF_KB, B_W, gkb_ref, 1.0, by_residue(kb_refs))
    for g in range(B_GROUPS):
        t = _dot(h, w_ref[:, OFF_VB + g * B_GROUP_W: OFF_VB + (g + 1) * B_GROUP_W])
        by_residue(vb_refs)(g, [t[:, k * LANES:(k + 1) * LANES] for k in range(B_GROUP_W // LANES)])
    for c in range(2 * D_MODEL // 512):
        g = _dot(h, w_ref[:, OFF_GA + c * 512: OFF_GA + (c + 1) * 512])
        gate_ref[:, c * 512:(c + 1) * 512] = jax.nn.sigmoid(g)


def _inproj(x2d, seq, n1, w_in, gqa, gka, gqb, gkb, cos_t, sin_t, bd):
    tokens = x2d.shape[0]
    tm = TM_IN
    blocks_per_seq = seq // tm
    row = lambda i: (i, 0)
    const = lambda i: (0, 0)
    pos = lambda i: (i % blocks_per_seq, 0)
    batch = tokens // seq
    group_shapes = [jax.ShapeDtypeStruct((batch, dil, seq // dil, B_GROUP_W), BF16) for _, dil in B_PATTERNS]
    group_specs = [pl.BlockSpec((None, dil, tm // dil, B_GROUP_W),
                                lambda i: (i // blocks_per_seq, 0, i % blocks_per_seq, 0)) for _, dil in B_PATTERNS]
    out_shape = (
        jax.ShapeDtypeStruct((tokens, A_Q_W), BF16),
        jax.ShapeDtypeStruct((tokens, A_KV_W), BF16),
        jax.ShapeDtypeStruct((tokens, A_KV_W), BF16),
        *group_shapes, *group_shapes, *group_shapes,
        jax.ShapeDtypeStruct((tokens, 2 * D_MODEL), F32),
    )
    return pl.pallas_call(
        _inproj_body,
        out_shape=out_shape,
        scratch_shapes=[pltpu.VMEM((B_GROUP_W // LANES, tm, LANES), F32)],
        grid=(tokens // tm,),
        in_specs=[
            pl.BlockSpec((tm, D_MODEL), row),
            pl.BlockSpec((1, D_MODEL), const),
            pl.BlockSpec((D_MODEL, IN_COLS), const),
            pl.BlockSpec((1, LANES), const),
            pl.BlockSpec((1, LANES), const),
            pl.BlockSpec((1, LANES), const),
            pl.BlockSpec((1, LANES), const),
            pl.BlockSpec((tm, LANES), pos),
            pl.BlockSpec((tm, LANES), pos),
            pl.BlockSpec((PREP_COLS, PREP_COLS), const),
        ],
        out_specs=(
            pl.BlockSpec((tm, A_Q_W), row),
            pl.BlockSpec((tm, A_KV_W), row),
            pl.BlockSpec((tm, A_KV_W), row),
            *group_specs, *group_specs, *group_specs,
            pl.BlockSpec((tm, 2 * D_MODEL), row),
        ),
        compiler_params=pltpu.CompilerParams(
            dimension_semantics=("arbitrary",), vmem_limit_bytes=VMEM_LIMIT),
        name="inproj",
    )(x2d, n1, w_in, gqa, gka, gqb, gkb, cos_t, sin_t, bd)


def _attn_body(*refs, window, tq, kw, length, nkv, grp, has_sink, want_lse):
    refs = list(refs)
    sink_ref = refs.pop(0) if has_sink else None
    q_ref, k_ref, v_ref, o_ref = refs[:4]
    lse_ref = refs[4] if want_lse else None

    q0 = pl.program_id(1) * tq
    if kw == length:
        start = 0
        kwin = k_ref[...]
        vwin = v_ref[...]
    else:
        start = pl.multiple_of(jnp.clip(q0 - window, 0, length - kw), window)
        kwin = k_ref[pl.ds(start, kw), :]
        vwin = v_ref[pl.ds(start, kw), :]
    sq = min(tq, ATTN_SUBQ) if grp > 1 else tq
    qi = lax.broadcasted_iota(I32, (sq, kw), 0)
    kj = lax.broadcasted_iota(I32, (sq, kw), 1)
    rel = kj - qi + (start - q0)
    valid = [jnp.abs(rel - r0) <= window for r0 in range(0, tq, sq)]

    o_parts = [None] * (nkv * grp)
    lse_parts = [None] * (nkv * grp)
    for j in range(nkv):
        kh = kwin[:, j * HEAD_DIM:(j + 1) * HEAD_DIM]
        vh = vwin[:, j * HEAD_DIM:(j + 1) * HEAD_DIM]
        for g in range(grp):
            h = j * grp + g
            o_rows, lse_rows = [], []
            for i, r0 in enumerate(range(0, tq, sq)):
                qh = q_ref[r0:r0 + sq, h * HEAD_DIM:(h + 1) * HEAD_DIM]
                s = jnp.where(valid[i], _dot_nt(qh, kh), MASK_VALUE)
                m = jnp.max(s, axis=-1, keepdims=True)
                if has_sink:
                    m = jnp.maximum(m, sink_ref[h])
                p = jnp.exp(s - m)
                denom = jnp.sum(p, axis=-1, keepdims=True)
                if has_sink:
                    denom = denom + jnp.exp(sink_ref[h] - m)
                o_rows.append(_dot(p.astype(BF16), vh) / denom)
                if want_lse:
                    lse_rows.append(jnp.broadcast_to(m + jnp.log(denom), (sq, HEAD_DIM)))
            o_parts[h] = o_rows[0] if len(o_rows) == 1 else jnp.concatenate(o_rows, axis=0)
            if want_lse:
                lse_parts[h] = lse_rows[0] if len(lse_rows) == 1 else jnp.concatenate(lse_rows, axis=0)
    o_ref[...] = jnp.concatenate(o_parts, axis=-1).astype(o_ref.dtype)
    if want_lse:
        lse_ref[...] = jnp.concatenate(lse_parts, axis=-1)


def _banded_attention(q, k, v, *, window, nkv, grp, sink=None, want_lse=False, out_dtype=BF16, name):
    nseq, length, wq = q.shape
    wk = nkv * HEAD_DIM
    tq = min(ATTN_TQ, length)
    kw = min(length, tq + 2 * window)
    body = functools.partial(_attn_body, window=window, tq=tq, kw=kw, length=length, nkv=nkv, grp=grp,
                             has_sink=sink is not None, want_lse=want_lse)
    in_specs = [
        pl.BlockSpec((None, tq, wq), lambda b, t: (b, t, 0)),
        pl.BlockSpec((None, length, wk), lambda b, t: (b, 0, 0)),
        pl.BlockSpec((None, length, wk), lambda b, t: (b, 0, 0)),
    ]
    args = [q, k, v]
    if sink is not None:
        in_specs.insert(0, pl.BlockSpec(memory_space=pltpu.SMEM))
        args.insert(0, sink)
    o_spec = pl.BlockSpec((None, tq, wq), lambda b, t: (b, t, 0))
    out_shape = [jax.ShapeDtypeStruct((nseq, length, wq), out_dtype)]
    out_specs = [o_spec]
    if want_lse:
        out_shape.append(jax.ShapeDtypeStruct((nseq, length, wq), F32))
        out_specs.append(o_spec)
    return pl.pallas_call(
        body,
        out_shape=tuple(out_shape),
        grid=(nseq, length // tq),
        in_specs=in_specs,
        out_specs=tuple(out_specs),
        compiler_params=pltpu.CompilerParams(
            dimension_semantics=("arbitrary", "arbitrary"), vmem_limit_bytes=VMEM_LIMIT),
        name=name,
    )(*args)


def _post_body(oa_ref, o1_ref, o2_ref, o3_ref, l1_ref, l2_ref, l3_ref, ga_ref, gb_ref, x_ref,
               woa_ref, wob_ref, wout_ref, n2_ref, wq_ref,
               x1_ref, h2_ref, qp_ref, scr):
    tm = x_ref.shape[0]

    def tokens_major(ref, g):
        dil = B_PATTERNS[g][1]
        if dil == 1:
            return ref[0]
        halves = []
        for k in range(B_GROUP_W // LANES):
            for r in range(dil):
                scr[k, pl.ds(r, tm // dil, stride=dil), :] = ref[r, :, k * LANES:(k + 1) * LANES]
            halves.append(scr[k])
        return jnp.concatenate(halves, axis=-1)

    l1, l2, l3 = [tokens_major(ref, g) for g, ref in enumerate((l1_ref, l2_ref, l3_ref))]
    mx = jnp.maximum(jnp.maximum(l1, l2), l3)
    e1, e2, e3 = jnp.exp(l1 - mx), jnp.exp(l2 - mx), jnp.exp(l3 - mx)
    tot = e1 + e2 + e3
    o1, o2, o3 = [tokens_major(ref, g) for g, ref in enumerate((o1_ref, o2_ref, o3_ref))]
    ob = (e1 / tot) * o1 + (e2 / tot) * o2 + (e3 / tot) * o3
    ya = _dot(oa_ref[...], woa_ref[...])
    yb = _dot(ob.astype(BF16), wob_ref[...])
    merged = ga_ref[...] * ya + gb_ref[...] * yb
    x1 = x_ref[...] + _dot(merged.astype(BF16), wout_ref[...])
    x1_ref[...] = x1
    h2 = _rms(x1, n2_ref[...]).astype(BF16)
    h2_ref[...] = h2
    for c in range(PEER_Q_W // PEER_HALF):
        qp_ref[c] = _dot(h2, wq_ref[:, c * PEER_HALF:(c + 1) * PEER_HALF]).astype(qp_ref.dtype)


def _post(oa, o_groups, lse_groups, gates, x2d, seq, w_o_a, w_o_b, w_out, n2, w_query):
    tokens = x2d.shape[0]
    tm = TM_POST
    blocks_per_seq = seq // tm
    row = lambda i: (i, 0)
    const = lambda i: (0, 0)
    nq = PEER_Q_W // PEER_HALF
    group_specs = [pl.BlockSpec((None, dil, tm // dil, B_GROUP_W),
                                lambda i: (i // blocks_per_seq, 0, i % blocks_per_seq, 0)) for _, dil in B_PATTERNS]
    return pl.pallas_call(
        _post_body,
        out_shape=(
            jax.ShapeDtypeStruct((tokens, D_MODEL), F32),
            jax.ShapeDtypeStruct((tokens, D_MODEL), BF16),
            jax.ShapeDtypeStruct((nq, tokens, PEER_HALF), BF16),
        ),
        grid=(tokens // tm,),
        scratch_shapes=[pltpu.VMEM((B_GROUP_W // LANES, tm, LANES), F32)],
        in_specs=[pl.BlockSpec((tm, A_Q_W), row)] + group_specs + group_specs
        + [
            pl.BlockSpec((tm, D_MODEL), lambda i: (i, 0)),
            pl.BlockSpec((tm, D_MODEL), lambda i: (i, 1)),
            pl.BlockSpec((tm, D_MODEL), row),
            pl.BlockSpec((A_Q_W, D_MODEL), const),
            pl.BlockSpec((B_GROUP_W, D_MODEL), const),
            pl.BlockSpec((D_MODEL, D_MODEL), const),
            pl.BlockSpec((1, D_MODEL), const),
            pl.BlockSpec((D_MODEL, PEER_Q_W), const),
        ],
        out_specs=(
            pl.BlockSpec((tm, D_MODEL), row),
            pl.BlockSpec((tm, D_MODEL), row),
            pl.BlockSpec((nq, tm, PEER_HALF), lambda i: (0, i, 0)),
        ),
        compiler_params=pltpu.CompilerParams(
            dimension_semantics=("arbitrary",), vmem_limit_bytes=VMEM_LIMIT),
        name="post",
    )(oa, *o_groups, *lse_groups, gates, gates, x2d, w_o_a, w_o_b, w_out, n2, w_query)


def _tree(xs, op):
    xs = list(xs)
    while len(xs) > 1:
        nxt = [op(xs[i], xs[i + 1]) for i in range(0, len(xs) - 1, 2)]
        if len(xs) % 2:
            nxt.append(xs[-1])
        xs = nxt
    return xs[0]


def _all_rows(blocks, op):
    r = _tree(blocks, op)
    for shift in (4, 2, 1):
        r = op(r, pltpu.roll(r, shift, 0))
    return r


def _pack(rows, sub):
    blk = rows[0]
    for i in range(1, SUBLANES):
        blk = jnp.where(sub == i, rows[i], blk)
    return blk


def _select_top(blocks, ids, payload, k):
    past = float(np.finfo(np.float32).max)
    out = []
    for _ in range(k):
        m = _all_rows(blocks, jnp.maximum)
        pos = _all_rows([jnp.where(b == m, i, past) for b, i in zip(blocks, ids)], jnp.minimum)
        sel = [i == pos for i in ids]
        picked = None
        if payload is not None:
            picked = _all_rows([jnp.where(s, p, -1.0) for s, p in zip(sel, payload)], jnp.maximum)
        blocks = [jnp.where(s, NEG, b) for s, b in zip(sel, blocks)]
        out.append((m, pos, picked))
    return out


def _candidate_rows(first, second, combine, sub):
    half = SUBLANES
    first_hi = pltpu.roll(_pack(first[half:], sub), 4, 0)
    lo = _pack(second[:half], sub)
    hi = _pack(second[half:], sub)
    w = jnp.where
    a_blocks = [first[0], first[0], first[1],
                w(sub < 5, first[2], first[5]),
                w(sub < 4, first[3], first[4]),
                w(sub < 2, first[6], w(sub < 4, first[7], first_hi)),
                first_hi]
    b_blocks = [lo, hi, lo,
                w(sub < 5, lo, pltpu.roll(lo, 5, 0)),
                w(sub < 4, lo, pltpu.roll(lo, 4, 0)),
                w(sub < 2, lo, w(sub < 4, pltpu.roll(lo, 2, 0), second[0])),
                second[0]]
    t = PEER_TOPK
    flat = [sub, half + sub, t + sub,
            w(sub < 5, 2 * t + sub, 5 * t + sub - 5),
            w(sub < 4, 3 * t + sub, 4 * t + sub - 4),
            w(sub < 2, 6 * t + sub, w(sub < 4, 7 * t + sub - 2, (half + sub - 4) * t)),
            (half + 4 + sub) * t]
    used = [None, None, None, sub < 7, sub < 7, None, sub < 4]
    flat = [f if u is None else w(u, f, -1) for f, u in zip(flat, used)]
    return [combine(a, b) for a, b in zip(a_blocks, b_blocks)], flat


def _head_experts(s1, s2):
    tm = s1.shape[1]
    sub = lax.broadcasted_iota(I32, (SUBLANES, tm), 0)
    subf = sub.astype(F32)
    key_ids = [subf + float(SUBLANES * g) for g in range(PEER_KEYS // SUBLANES)]

    def half(s):
        top = _select_top([s[SUBLANES * g:SUBLANES * (g + 1)] for g in range(PEER_KEYS // SUBLANES)], key_ids,
                          None, PEER_TOPK)
        return [t[0] for t in top], [t[1] for t in top]

    v1, i1 = half(s1)
    v2, i2 = half(s2)
    cand, flat = _candidate_rows(v1, v2, lambda a, b: a + b, sub)
    cand = [jnp.where(f >= 0, c, NEG) for c, f in zip(cand, flat)]
    expert, _ = _candidate_rows(i1, i2, lambda a, b: a * float(PEER_KEYS) + b, sub)
    unused_base = PEER_TOPK * PEER_TOPK
    ids = [jnp.where(f >= 0, f, unused_base + SUBLANES * g + sub).astype(F32) for g, f in enumerate(flat)]
    top = _select_top(cand, ids, expert, PEER_TOPK)
    ex = [jnp.exp(_pack([t[0] for t in top[o:o + SUBLANES]], sub) - top[0][0]) for o in (0, SUBLANES)]
    tot = _all_rows(ex, jnp.add)
    out = []
    for o, e in zip((0, SUBLANES), ex):
        code = _pack([t[2] for t in top[o:o + SUBLANES]], sub)
        e1 = jnp.floor(code * (1.0 / PEER_KEYS))
        out.append((e / tot, e1, code - e1 * float(PEER_KEYS)))
    return out


def _peer_head(h, qp_ref, k1_ref, k2_ref, e1t_scr, e2t_scr, gt_scr):
    s1 = _dot_nt(k1_ref[h], qp_ref[2 * h])
    s2 = _dot_nt(k2_ref[h], qp_ref[2 * h + 1])
    base = pl.multiple_of(h * PEER_TOPK, PEER_TOPK)
    for o, (gate, e1, e2) in zip((0, SUBLANES), _head_experts(s1, s2)):
        rows = pl.ds(pl.multiple_of(base + o, SUBLANES), SUBLANES)
        gt_scr[rows, :] = gate
        e1t_scr[rows, :] = e1
        e2t_scr[rows, :] = e2


def _topk_body(qp_ref, k1_ref, k2_ref, e1_ref, e2_ref, g_ref, e1t_scr, e2t_scr, gt_scr):
    def head_group(i, carry):
        for o in range(TOPK_HEADS_PER_ITER):
            _peer_head(TOPK_HEADS_PER_ITER * i + o, qp_ref, k1_ref, k2_ref, e1t_scr, e2t_scr, gt_scr)
        return carry

    lax.fori_loop(0, PEER_HEADS // TOPK_HEADS_PER_ITER, head_group, 0)
    e1_ref[...] = e1t_scr[...].T.astype(I32)
    e2_ref[...] = e2t_scr[...].T.astype(I32)
    g_ref[...] = gt_scr[...].T


def _topk(qp, k1, k2):
    nq, tokens, _ = qp.shape
    tm = TM_TOPK
    assert tm == PEER_SLOTS
    row = lambda i: (i, 0)
    const3 = lambda i: (0, 0, 0)
    return pl.pallas_call(
        _topk_body,
        out_shape=(
            jax.ShapeDtypeStruct((tokens, PEER_SLOTS), I32),
            jax.ShapeDtypeStruct((tokens, PEER_SLOTS), I32),
            jax.ShapeDtypeStruct((tokens, PEER_SLOTS), F32),
        ),
        grid=(tokens // tm,),
        in_specs=[
            pl.BlockSpec((nq, tm, PEER_HALF), lambda i: (0, i, 0)),
            pl.BlockSpec((PEER_HEADS, PEER_KEYS, PEER_HALF), const3),
            pl.BlockSpec((PEER_HEADS, PEER_KEYS, PEER_HALF), const3),
        ],
        out_specs=(
            pl.BlockSpec((tm, PEER_SLOTS), row),
            pl.BlockSpec((tm, PEER_SLOTS), row),
            pl.BlockSpec((tm, PEER_SLOTS), row),
        ),
        scratch_shapes=[
            pltpu.VMEM((PEER_SLOTS, tm), F32),
            pltpu.VMEM((PEER_SLOTS, tm), F32),
            pltpu.VMEM((PEER_SLOTS, tm), F32),
        ],
        compiler_params=pltpu.CompilerParams(
            dimension_semantics=("arbitrary",), vmem_limit_bytes=VMEM_LIMIT),
        name="topk",
    )(qp, k1, k2)


def _peer_u_body(h2_ref, ut_ref, e1_ref, e2_ref, g_ref, w_ref, acc_ref, sc_ref):
    j = pl.program_id(1)
    n_e1 = EC_U // PEER_KEYS

    @pl.when(j == 0)
    def _():
        acc_ref[...] = jnp.zeros_like(acc_ref)
        sc_ref[...] = jnp.zeros_like(sc_ref)

    prev = sc_ref.at[(j + 1) % 2]
    for r in range(0, h2_ref.shape[0], GATHER_ROWS):
        rows = slice(r, r + GATHER_ROWS)
        e1 = e1_ref[rows, :]
        e2 = e2_ref[rows, :]
        acc = acc_ref[rows, :]
        for c in range(n_e1):
            picked = jnp.take_along_axis(prev[rows, c * PEER_KEYS:(c + 1) * PEER_KEYS], e2, axis=1)
            acc = jnp.where(e1 == (j - 1) * n_e1 + c, picked, acc)
        acc_ref[rows, :] = acc
    sc_ref[j % 2] = _dot(h2_ref[...], ut_ref[...])

    @pl.when(j == pl.num_programs(1) - 1)
    def _():
        acc = acc_ref[...]
        w_ref[...] = g_ref[...] * (0.5 * acc * (1.0 + lax.erf(acc * np.float32(np.sqrt(0.5)))))


def _peer_u(h2, u_t, e1, e2, gate):
    tokens = h2.shape[0]
    tm = TM_U
    n_chunks = PEER_EXPERTS // EC_U
    row = lambda i, j: (i, 0)
    return pl.pallas_call(
        _peer_u_body,
        out_shape=jax.ShapeDtypeStruct((tokens, PEER_SLOTS), F32),
        grid=(tokens // tm, n_chunks + 1),
        in_specs=[
            pl.BlockSpec((tm, D_MODEL), row),
            pl.BlockSpec((D_MODEL, EC_U), lambda i, j: (0, jnp.minimum(j, n_chunks - 1))),
            pl.BlockSpec((tm, PEER_SLOTS), row),
            pl.BlockSpec((tm, PEER_SLOTS), row),
            pl.BlockSpec((tm, PEER_SLOTS), row),
        ],
        out_specs=pl.BlockSpec((tm, PEER_SLOTS), row),
        scratch_shapes=[pltpu.VMEM((tm, PEER_SLOTS), F32), pltpu.VMEM((2, tm, EC_U), F32)],
        compiler_params=pltpu.CompilerParams(
            dimension_semantics=("arbitrary", "arbitrary"), vmem_limit_bytes=VMEM_LIMIT),
        name="peer_u",
    )(h2, u_t, e1, e2, gate)


def _wd_pitch(tm):
    tiles = tm // 2 // SUBLANES
    return (tiles + 1 - tiles % 2) * SUBLANES


def _bf16_bits(x):
    return lax.bitcast_convert_type(x.astype(BF16).astype(F32), U32)


def _peer_v_body(e1_ref, e2_ref, w_ref, v_ref, x1_ref, y_ref, wd_scr):
    tm = e1_ref.shape[0]
    half = tm // 2
    pitch = _wd_pitch(tm)
    j = pl.program_id(1)

    @pl.when(j == 0)
    def _():
        eiota = lax.broadcasted_iota(I32, (PEER_KEYS, PEER_SLOTS), 0)

        def tile(t):
            e1 = e1_ref[pl.ds(t, 1), :]
            e2 = e2_ref[pl.ds(t, 1), :]
            wt = w_ref[pl.ds(t, 1), :]
            a = jnp.where(eiota == e1, wt, 0.0).astype(BF16)
            b = jnp.where(eiota == e2, 1.0, 0.0).astype(BF16)
            return _dot_nt(a, b)

        def token_pair(p, carry):
            packed = (_bf16_bits(tile(p)) >> 16) | _bf16_bits(tile(p + half))
            wd_scr[pl.ds(p, PEER_KEYS, stride=pitch), :] = packed
            return carry

        lax.fori_loop(0, half, token_pair, 0, unroll=32)
        y_ref[...] = x1_ref[...]

    n_e1 = EC_V // PEER_KEYS
    parts = []
    for c in range(n_e1):
        words = wd_scr[pl.ds(pl.multiple_of((j * n_e1 + c) * pitch, SUBLANES), half), :]
        lo = lax.bitcast_convert_type(words << 16, F32)
        hi = lax.bitcast_convert_type(words & jnp.uint32(0xFFFF0000), F32)
        parts.append(jnp.concatenate([lo, hi], axis=0))
    dense_w = jnp.concatenate(parts, axis=-1).astype(BF16)
    y_ref[...] += _dot(dense_w, v_ref[...])


def _peer_v(e1, e2, w, v_bf16, x1):
    tokens = x1.shape[0]
    tm = TM_V
    row = lambda i, j: (i, 0)
    return pl.pallas_call(
        _peer_v_body,
        out_shape=jax.ShapeDtypeStruct((tokens, D_MODEL), F32),
        grid=(tokens // tm, PEER_EXPERTS // EC_V),
        in_specs=[
            pl.BlockSpec((tm, PEER_SLOTS), row),
            pl.BlockSpec((tm, PEER_SLOTS), row),
            pl.BlockSpec((tm, PEER_SLOTS), row),
            pl.BlockSpec((EC_V, D_MODEL), lambda i, j: (j, 0)),
            pl.BlockSpec((tm, D_MODEL), row, pipeline_mode=pl.Buffered(1)),
        ],
        out_specs=pl.BlockSpec((tm, D_MODEL), row, pipeline_mode=pl.Buffered(1)),
        scratch_shapes=[pltpu.VMEM((PEER_KEYS * _wd_pitch(tm), PEER_KEYS), U32)],
        compiler_params=pltpu.CompilerParams(
            dimension_semantics=("arbitrary", "arbitrary"), vmem_limit_bytes=VMEM_LIMIT),
        name="peer_v",
    )(e1, e2, w, v_bf16, x1)


def _rope_tables(seq):
    inv = ROPE_THETA ** (-jnp.arange(0, ROPE_DIM, 2, dtype=F32) / ROPE_DIM)
    ang = jnp.arange(seq, dtype=F32)[:, None] * inv[None, :]
    cos = jnp.cos(ang)
    sin = jnp.sin(ang)
    lane = np.arange(LANES)
    in_head = lane % HEAD_DIM
    freq = in_head % ROPE_HALF
    cos_l = jnp.where(in_head < ROPE_DIM, cos[:, freq], 1.0)
    sin_l = jnp.where(in_head < ROPE_HALF, -sin[:, freq], jnp.where(in_head < ROPE_DIM, sin[:, freq], 0.0))
    return cos_l.astype(F32), sin_l.astype(F32)


def _layer(x, p):
    batch, seq, _ = x.shape
    tokens = batch * seq
    x2d = x.reshape(tokens, D_MODEL)
    cos_t, sin_t = _rope_tables(seq)
    outs = _inproj(x2d, seq, p["norm1"], p["w_in"], p["gqa"], p["gka"], p["gqb"], p["gkb"], cos_t, sin_t, p["bd"])
    qa, ka, va = outs[:3]
    qb, kb, vb = outs[3:3 + B_GROUPS], outs[3 + B_GROUPS:3 + 2 * B_GROUPS], outs[3 + 2 * B_GROUPS:3 + 3 * B_GROUPS]
    gates = outs[3 + 3 * B_GROUPS]

    oa, = _banded_attention(
        qa.reshape(batch, seq, A_Q_W), ka.reshape(batch, seq, A_KV_W), va.reshape(batch, seq, A_KV_W),
        window=A_WINDOW, nkv=A_KV_HEADS, grp=A_GROUP, sink=p["sink"], name="attn_a")

    o_groups, lse_groups = [], []
    for gi, (win, dil) in enumerate(B_PATTERNS):
        sub = seq // dil
        as_seqs = lambda t: t.reshape(batch * dil, sub, B_GROUP_W)
        o, lse = _banded_attention(
            as_seqs(qb[gi]), as_seqs(kb[gi]), as_seqs(vb[gi]), window=win // (2 * dil),
            nkv=B_HEADS_PER_GROUP, grp=1, want_lse=True, out_dtype=F32, name=f"attn_b{gi}")
        o_groups.append(o.reshape(batch, dil, sub, B_GROUP_W))
        lse_groups.append(lse.reshape(batch, dil, sub, B_GROUP_W))

    x1, h2, qp = _post(oa.reshape(tokens, A_Q_W), o_groups, lse_groups, gates, x2d, seq,
                       p["w_o_a"], p["w_o_b"], p["w_out"], p["norm2"], p["w_query"])
    e1, e2, gate = _topk(qp, p["k1"], p["k2"])
    w = _peer_u(h2, p["u_t"], e1, e2, gate)
    y = _peer_v(e1, e2, w, p["v"], x1)
    return y.reshape(batch, seq, D_MODEL)


def _tile_gain(g):
    return jnp.tile(g.astype(F32), LANES // HEAD_DIM).reshape(1, LANES)


def kernel(x_prompt, x_sample, norm1, w_in, q_norm_a, k_norm_a, sink_a, q_norm_b, k_norm_b, w_o_a, w_o_b, w_out,
           norm2, w_query, sub_keys_1, sub_keys_2, expert_u, expert_v):
    depth = norm1.shape[0]
    head_of_lane = np.arange(PREP_COLS) // HEAD_DIM
    bd = jnp.asarray((head_of_lane[:, None] == head_of_lane[None, :]) / HEAD_DIM, dtype=BF16)
    params = []
    for l in range(depth):
        params.append({
            "norm1": norm1[l].reshape(1, D_MODEL),
            "w_in": w_in[l].astype(BF16),
            "gqa": _tile_gain(q_norm_a[l]), "gka": _tile_gain(k_norm_a[l]),
            "gqb": _tile_gain(q_norm_b[l]), "gkb": _tile_gain(k_norm_b[l]),
            "sink": sink_a[l].astype(F32),
            "w_o_a": w_o_a[l].astype(BF16), "w_o_b": w_o_b[l].astype(BF16), "w_out": w_out[l].astype(BF16),
            "norm2": norm2[l].reshape(1, D_MODEL),
            "w_query": w_query[l].astype(BF16),
            "k1": sub_keys_1[l].astype(BF16), "k2": sub_keys_2[l].astype(BF16),
            "u_t": expert_u[l].astype(BF16).T,
            "v": expert_v[l].astype(BF16),
            "bd": bd,
        })
    ys = []
    for x in (x_prompt, x_sample):
        for p in params:
            x = _layer(x, p)
        ys.append(x)
    return tuple(ys)
```

```python
import functools

import numpy as np
import jax
import jax.numpy as jnp
from jax import lax
from jax.experimental import pallas as pl
from jax.experimental.pallas import tpu as pltpu

F32 = jnp.float32
BF16 = jnp.bfloat16
I32 = jnp.int32
U32 = jnp.uint32

D_MODEL = 1024
HEAD_DIM = 64
ROPE_DIM = HEAD_DIM // 4
ROPE_HALF = ROPE_DIM // 2
ROPE_THETA = 500000.0
EPS = 1e-6
A_Q_HEADS = 8
A_KV_HEADS = 2
A_GROUP = A_Q_HEADS // A_KV_HEADS
A_WINDOW = 128
B_PATTERNS = ((128, 1), (512, 4), (2048, 16))
B_HEADS_PER_GROUP = 4
B_GROUPS = len(B_PATTERNS)
A_Q_W = A_Q_HEADS * HEAD_DIM
A_KV_W = A_KV_HEADS * HEAD_DIM
B_GROUP_W = B_HEADS_PER_GROUP * HEAD_DIM
B_W = B_GROUPS * B_GROUP_W
IN_COLS = A_Q_W + 2 * A_KV_W + 3 * B_W + 2 * D_MODEL
PEER_HEADS = 8
PEER_KEYS = 128
PEER_EXPERTS = PEER_KEYS * PEER_KEYS
PEER_HALF = 128
PEER_TOPK = 16
PEER_SLOTS = PEER_HEADS * PEER_TOPK
PEER_Q_W = PEER_HEADS * 2 * PEER_HALF

LANES = 128
SUBLANES = 8
MASK_VALUE = -1e30
NEG = float(np.finfo(np.float32).min)
VMEM_LIMIT = 56 * 1024 * 1024

OFF_QA = 0
OFF_KA = OFF_QA + A_Q_W
OFF_VA = OFF_KA + A_KV_W
OFF_QB = OFF_VA + A_KV_W
OFF_KB = OFF_QB + B_W
OFF_VB = OFF_KB + B_W
OFF_GA = OFF_VB + B_W

TM_IN = 512
PREP_COLS = 256
TM_POST = 512
TM_TOPK = 128
TOPK_HEADS_PER_ITER = 8
TM_U = 2048
EC_U = 512
GATHER_ROWS = 8
TM_V = 1024
EC_V = 1024
ATTN_TQ = 256
ATTN_SUBQ = 256

def _dot(a, b):
    return jnp.dot(a, b, preferred_element_type=F32)


def _dot_nt(a, b):
    return lax.dot_general(a, b, (((1,), (1,)), ((), ())), preferred_element_type=F32)


def _rms(x, g):
    ms = jnp.mean(x * x, axis=-1, keepdims=True)
    return (x * lax.rsqrt(ms + EPS)) * g


def _head_prep(t, gain, cosv, sinv, bd, lane_lo):
    sq = t * t
    hi = sq.astype(BF16)
    lo = (sq - hi.astype(F32)).astype(BF16)
    ms = _dot(hi, bd) + _dot(lo, bd)
    tn = (t * lax.rsqrt(ms + EPS)) * gain
    out = []
    for c in range(t.shape[1] // LANES):
        tc = tn[:, c * LANES:(c + 1) * LANES]
        partner = jnp.where(lane_lo, pltpu.roll(tc, LANES - ROPE_HALF, 1), pltpu.roll(tc, ROPE_HALF, 1))
        out.append(tc * cosv + partner * sinv)
    return out


def _inproj_body(x_ref, n1_ref, w_ref, gqa_ref, gka_ref, gqb_ref, gkb_ref, cos_ref, sin_ref, bd_ref,
                 qa_ref, ka_ref, va_ref, *rest):
    qb_refs, kb_refs, vb_refs = rest[0:B_GROUPS], rest[B_GROUPS:2 * B_GROUPS], rest[2 * B_GROUPS:3 * B_GROUPS]
    gate_ref, scr = rest[3 * B_GROUPS], rest[3 * B_GROUPS + 1]
    tm = x_ref.shape[0]
    h = _rms(x_ref[...], n1_ref[...]).astype(BF16)
    cosv = cos_ref[...]
    sinv = sin_ref[...]
    bd = bd_ref[...]
    lane = lax.broadcasted_iota(I32, (1, LANES), 1)
    lane_lo = (lane % HEAD_DIM) < ROPE_HALF

    def prepped(off, width, gain_ref, scale, emit):
        step = min(width, PREP_COLS)
        gain = jnp.concatenate([gain_ref[...]] * (step // LANES), axis=-1)
        for c in range(width // step):
            t = _dot(h, w_ref[:, off + c * step: off + (c + 1) * step])
            chunks = _head_prep(t, gain, cosv, sinv, bd[:step, :step], lane_lo)
            emit(c, [tk * scale if scale != 1.0 else tk for tk in chunks])

    def flat(out_ref):
        def emit(c, chunks):
            for k, tk in enumerate(chunks):
                lo = c * PREP_COLS + k * LANES
                out_ref[:, lo:lo + LANES] = tk.astype(out_ref.dtype)
        return emit

    def by_residue(out_refs):
        def emit(g, chunks):
            dil = B_PATTERNS[g][1]
            for k, tk in enumerate(chunks):
                cols = slice(k * LANES, (k + 1) * LANES)
                if dil == 1:
                    out_refs[g][0, :, cols] = tk.astype(out_refs[g].dtype)
                    continue
                scr[k] = tk
                for r in range(dil):
                    rows = scr[k, pl.ds(r, tm // dil, stride=dil), :]
                    out_refs[g][r, :, cols] = rows.astype(out_refs[g].dtype)
        return emit

    assert PREP_COLS == B_GROUP_W
    qscale = HEAD_DIM ** -0.5
    prepped(OFF_QA, A_Q_W, gqa_ref, qscale, flat(qa_ref))
    prepped(OFF_KA, A_KV_W, gka_ref, 1.0, flat(ka_ref))
    va_ref[...] = _dot(h, w_ref[:, OFF_VA:OFF_VA + A_KV_W]).astype(va_ref.dtype)
    prepped(OFF_QB, B_W, gqb_ref, qscale, by_residue(qb_refs))
    prepped(OFF_KB, B_W, gkb_ref, 1.0, by_residue(kb_refs))
    for g in range(B_GROUPS):
        t = _dot(h, w_ref[:, OFF_VB + g * B_GROUP_W: OFF_VB + (g + 1) * B_GROUP_W])
        by_residue(vb_refs)(g, [t[:, k * LANES:(k + 1) * LANES] for k in range(B_GROUP_W // LANES)])
    for c in range(2 * D_MODEL // 512):
        g = _dot(h, w_ref[:, OFF_GA + c * 512: OFF_GA + (c + 1) * 512])
        gate_ref[:, c * 512:(c + 1) * 512] = jax.nn.sigmoid(g)


def _inproj(x2d, seq, n1, w_in, gqa, gka, gqb, gkb, cos_t, sin_t, bd):
    tokens = x2d.shape[0]
    tm = TM_IN
    blocks_per_seq = seq // tm
    row = lambda i: (i, 0)
    const = lambda i: (0, 0)
    pos = lambda i: (i % blocks_per_seq, 0)
    batch = tokens // seq
    group_shapes = [jax.ShapeDtypeStruct((batch, dil, seq // dil, B_GROUP_W), BF16) for _, dil in B_PATTERNS]
    group_specs = [pl.BlockSpec((None, dil, tm // dil, B_GROUP_W),
                                lambda i: (i // blocks_per_seq, 0, i % blocks_per_seq, 0)) for _, dil in B_PATTERNS]
    out_shape = (
        jax.ShapeDtypeStruct((tokens, A_Q_W), BF16),
        jax.ShapeDtypeStruct((tokens, A_KV_W), BF16),
        jax.ShapeDtypeStruct((tokens, A_KV_W), BF16),
        *group_shapes, *group_shapes, *group_shapes,
        jax.ShapeDtypeStruct((tokens, 2 * D_MODEL), F32),
    )
    return pl.pallas_call(
        _inproj_body,
        out_shape=out_shape,
        scratch_shapes=[pltpu.VMEM((B_GROUP_W // LANES, tm, LANES), F32)],
        grid=(tokens // tm,),
        in_specs=[
            pl.BlockSpec((tm, D_MODEL), row),
            pl.BlockSpec((1, D_MODEL), const),
            pl.BlockSpec((D_MODEL, IN_COLS), const),
            pl.BlockSpec((1, LANES), const),
            pl.BlockSpec((1, LANES), const),
            pl.BlockSpec((1, LANES), const),
            pl.BlockSpec((1, LANES), const),
            pl.BlockSpec((tm, LANES), pos),
            pl.BlockSpec((tm, LANES), pos),
            pl.BlockSpec((PREP_COLS, PREP_COLS), const),
        ],
        out_specs=(
            pl.BlockSpec((tm, A_Q_W), row),
            pl.BlockSpec((tm, A_KV_W), row),
            pl.BlockSpec((tm, A_KV_W), row),
            *group_specs, *group_specs, *group_specs,
            pl.BlockSpec((tm, 2 * D_MODEL), row),
        ),
        compiler_params=pltpu.CompilerParams(
            dimension_semantics=("arbitrary",), vmem_limit_bytes=VMEM_LIMIT),
        name="inproj",
    )(x2d, n1, w_in, gqa, gka, gqb, gkb, cos_t, sin_t, bd)


def _attn_body(*refs, window, tq, kw, length, nkv, grp, has_sink, want_lse):
    refs = list(refs)
    sink_ref = refs.pop(0) if has_sink else None
    q_ref, k_ref, v_ref, o_ref = refs[:4]
    lse_ref = refs[4] if want_lse else None

    q0 = pl.program_id(1) * tq
    if kw == length:
        start = 0
        kwin = k_ref[...]
        vwin = v_ref[...]
    else:
        start = pl.multiple_of(jnp.clip(q0 - window, 0, length - kw), window)
        kwin = k_ref[pl.ds(start, kw), :]
        vwin = v_ref[pl.ds(start, kw), :]
    sq = min(tq, ATTN_SUBQ)
    qi = lax.broadcasted_iota(I32, (sq, kw), 0)
    kj = lax.broadcasted_iota(I32, (sq, kw), 1)
    rel = kj - qi + (start - q0)
    valid = [jnp.abs(rel - r0) <= window for r0 in range(0, tq, sq)]

    o_parts = [None] * (nkv * grp)
    lse_parts = [None] * (nkv * grp)
    for j in range(nkv):
        kh = kwin[:, j * HEAD_DIM:(j + 1) * HEAD_DIM]
        vh = vwin[:, j * HEAD_DIM:(j + 1) * HEAD_DIM]
        for g in range(grp):
            h = j * grp + g
            o_rows, lse_rows = [], []
            for i, r0 in enumerate(range(0, tq, sq)):
                qh = q_ref[r0:r0 + sq, h * HEAD_DIM:(h + 1) * HEAD_DIM]
                s = jnp.where(valid[i], _dot_nt(qh, kh), MASK_VALUE)
                m = jnp.max(s, axis=-1, keepdims=True)
                if has_sink:
                    m = jnp.maximum(m, sink_ref[h])
                p = jnp.exp(s - m)
                denom = jnp.sum(p, axis=-1, keepdims=True)
                if has_sink:
                    denom = denom + jnp.exp(sink_ref[h] - m)
                o_rows.append(_dot(p.astype(BF16), vh) / denom)
                if want_lse:
                    lse_rows.append(jnp.broadcast_to(m + jnp.log(denom), (sq, HEAD_DIM)))
            o_parts[h] = o_rows[0] if len(o_rows) == 1 else jnp.concatenate(o_rows, axis=0)
            if want_lse:
                lse_parts[h] = lse_rows[0] if len(lse_rows) == 1 else jnp.concatenate(lse_rows, axis=0)
    o_ref[...] = jnp.concatenate(o_parts, axis=-1).astype(o_ref.dtype)
    if want_lse:
        lse_ref[...] = jnp.concatenate(lse_parts, axis=-1)


def _banded_attention(q, k, v, *, window, nkv, grp, sink=None, want_lse=False, out_dtype=BF16, name):
    nseq, length, wq = q.shape
    wk = nkv * HEAD_DIM
    tq = min(ATTN_TQ, length)
    kw = min(length, tq + 2 * window)
    body = functools.partial(_attn_body, window=window, tq=tq, kw=kw, length=length, nkv=nkv, grp=grp,
                             has_sink=sink is not None, want_lse=want_lse)
    in_specs = [
        pl.BlockSpec((None, tq, wq), lambda b, t: (b, t, 0)),
        pl.BlockSpec((None, length, wk), lambda b, t: (b, 0, 0)),
        pl.BlockSpec((None, length, wk), lambda b, t: (b, 0, 0)),
    ]
    args = [q, k, v]
    if sink is not None:
        in_specs.insert(0, pl.BlockSpec(memory_space=pltpu.SMEM))
        args.insert(0, sink)
    o_spec = pl.BlockSpec((None, tq, wq), lambda b, t: (b, t, 0))
    out_shape = [jax.ShapeDtypeStruct((nseq, length, wq), out_dtype)]
    out_specs = [o_spec]
    if want_lse:
        out_shape.append(jax.ShapeDtypeStruct((nseq, length, wq), F32))
        out_specs.append(o_spec)
    return pl.pallas_call(
        body,
        out_shape=tuple(out_shape),
        grid=(nseq, length // tq),
        in_specs=in_specs,
        out_specs=tuple(out_specs),
        compiler_params=pltpu.CompilerParams(
            dimension_semantics=("arbitrary", "arbitrary"), vmem_limit_bytes=VMEM_LIMIT),
        name=name,
    )(*args)


def _post_body(oa_ref, o1_ref, o2_ref, o3_ref, l1_ref, l2_ref, l3_ref, ga_ref, gb_ref, x_ref,
               woa_ref, wob_ref, wout_ref, n2_ref, wq_ref,
               x1_ref, h2_ref, qp_ref, scr):
    tm = x_ref.shape[0]

    def tokens_major(ref, g):
        dil = B_PATTERNS[g][1]
        if dil == 1:
            return ref[0]
        halves = []
        for k in range(B_GROUP_W // LANES):
            for r in range(dil):
                scr[k, pl.ds(r, tm // dil, stride=dil), :] = ref[r, :, k * LANES:(k + 1) * LANES]
            halves.append(scr[k])
        return jnp.concatenate(halves, axis=-1)

    l1, l2, l3 = [tokens_major(ref, g) for g, ref in enumerate((l1_ref, l2_ref, l3_ref))]
    mx = jnp.maximum(jnp.maximum(l1, l2), l3)
    e1, e2, e3 = jnp.exp(l1 - mx), jnp.exp(l2 - mx), jnp.exp(l3 - mx)
    tot = e1 + e2 + e3
    o1, o2, o3 = [tokens_major(ref, g) for g, ref in enumerate((o1_ref, o2_ref, o3_ref))]
    ob = (e1 / tot) * o1 + (e2 / tot) * o2 + (e3 / tot) * o3
    ya = _dot(oa_ref[...], woa_ref[...])
    yb = _dot(ob.astype(BF16), wob_ref[...])
    merged = ga_ref[...] * ya + gb_ref[...] * yb
    x1 = x_ref[...] + _dot(merged.astype(BF16), wout_ref[...])
    x1_ref[...] = x1
    h2 = _rms(x1, n2_ref[...]).astype(BF16)
    h2_ref[...] = h2
    for c in range(PEER_Q_W // PEER_HALF):
        qp_ref[c] = _dot(h2, wq_ref[:, c * PEER_HALF:(c + 1) * PEER_HALF]).astype(qp_ref.dtype)


def _post(oa, o_groups, lse_groups, gates, x2d, seq, w_o_a, w_o_b, w_out, n2, w_query):
    tokens = x2d.shape[0]
    tm = TM_POST
    blocks_per_seq = seq // tm
    row = lambda i: (i, 0)
    const = lambda i: (0, 0)
    nq = PEER_Q_W // PEER_HALF
    group_specs = [pl.BlockSpec((None, dil, tm // dil, B_GROUP_W),
                                lambda i: (i // blocks_per_seq, 0, i % blocks_per_seq, 0)) for _, dil in B_PATTERNS]
    return pl.pallas_call(
        _post_body,
        out_shape=(
            jax.ShapeDtypeStruct((tokens, D_MODEL), F32),
            jax.ShapeDtypeStruct((tokens, D_MODEL), BF16),
            jax.ShapeDtypeStruct((nq, tokens, PEER_HALF), BF16),
        ),
        grid=(tokens // tm,),
        scratch_shapes=[pltpu.VMEM((B_GROUP_W // LANES, tm, LANES), F32)],
        in_specs=[pl.BlockSpec((tm, A_Q_W), row)] + group_specs + group_specs
        + [
            pl.BlockSpec((tm, D_MODEL), lambda i: (i, 0)),
            pl.BlockSpec((tm, D_MODEL), lambda i: (i, 1)),
            pl.BlockSpec((tm, D_MODEL), row),
            pl.BlockSpec((A_Q_W, D_MODEL), const),
            pl.BlockSpec((B_GROUP_W, D_MODEL), const),
            pl.BlockSpec((D_MODEL, D_MODEL), const),
            pl.BlockSpec((1, D_MODEL), const),
            pl.BlockSpec((D_MODEL, PEER_Q_W), const),
        ],
        out_specs=(
            pl.BlockSpec((tm, D_MODEL), row),
            pl.BlockSpec((tm, D_MODEL), row),
            pl.BlockSpec((nq, tm, PEER_HALF), lambda i: (0, i, 0)),
        ),
        compiler_params=pltpu.CompilerParams(
            dimension_semantics=("arbitrary",), vmem_limit_bytes=VMEM_LIMIT),
        name="post",
    )(oa, *o_groups, *lse_groups, gates, gates, x2d, w_o_a, w_o_b, w_out, n2, w_query)


def _tree(xs, op):
    xs = list(xs)
    while len(xs) > 1:
        nxt = [op(xs[i], xs[i + 1]) for i in range(0, len(xs) - 1, 2)]
        if len(xs) % 2:
            nxt.append(xs[-1])
        xs = nxt
    return xs[0]


def _all_rows(blocks, op):
    r = _tree(blocks, op)
    for shift in (4, 2, 1):
        r = op(r, pltpu.roll(r, shift, 0))
    return r


def _pack(rows, sub):
    blk = rows[0]
    for i in range(1, SUBLANES):
        blk = jnp.where(sub == i, rows[i], blk)
    return blk


def _select_top(blocks, ids, payload, k):
    past = float(np.finfo(np.float32).max)
    out = []
    for _ in range(k):
        m = _all_rows(blocks, jnp.maximum)
        pos = _all_rows([jnp.where(b == m, i, past) for b, i in zip(blocks, ids)], jnp.minimum)
        sel = [i == pos for i in ids]
        picked = None
        if payload is not None:
            picked = _all_rows([jnp.where(s, p, -1.0) for s, p in zip(sel, payload)], jnp.maximum)
        blocks = [jnp.where(s, NEG, b) for s, b in zip(sel, blocks)]
        out.append((m, pos, picked))
    return out


def _candidate_rows(first, second, combine, sub):
    half = SUBLANES
    first_hi = pltpu.roll(_pack(first[half:], sub), 4, 0)
    lo = _pack(second[:half], sub)
    hi = _pack(second[half:], sub)
    w = jnp.where
    a_blocks = [first[0], first[0], first[1],
                w(sub < 5, first[2], first[5]),
                w(sub < 4, first[3], first[4]),
                w(sub < 2, first[6], w(sub < 4, first[7], first_hi)),
                first_hi]
    b_blocks = [lo, hi, lo,
                w(sub < 5, lo, pltpu.roll(lo, 5, 0)),
                w(sub < 4, lo, pltpu.roll(lo, 4, 0)),
                w(sub < 2, lo, w(sub < 4, pltpu.roll(lo, 2, 0), second[0])),
                second[0]]
    t = PEER_TOPK
    flat = [sub, half + sub, t + sub,
            w(sub < 5, 2 * t + sub, 5 * t + sub - 5),
            w(sub < 4, 3 * t + sub, 4 * t + sub - 4),
            w(sub < 2, 6 * t + sub, w(sub < 4, 7 * t + sub - 2, (half + sub - 4) * t)),
            (half + 4 + sub) * t]
    used = [None, None, None, sub < 7, sub < 7, None, sub < 4]
    flat = [f if u is None else w(u, f, -1) for f, u in zip(flat, used)]
    return [combine(a, b) for a, b in zip(a_blocks, b_blocks)], flat


def _head_experts(s1, s2):
    tm = s1.shape[1]
    sub = lax.broadcasted_iota(I32, (SUBLANES, tm), 0)
    subf = sub.astype(F32)
    key_ids = [subf + float(SUBLANES * g) for g in range(PEER_KEYS // SUBLANES)]

    def half(s):
        top = _select_top([s[SUBLANES * g:SUBLANES * (g + 1)] for g in range(PEER_KEYS // SUBLANES)], key_ids,
                          None, PEER_TOPK)
        return [t[0] for t in top], [t[1] for t in top]

    v1, i1 = half(s1)
    v2, i2 = half(s2)
    cand, flat = _candidate_rows(v1, v2, lambda a, b: a + b, sub)
    cand = [jnp.where(f >= 0, c, NEG) for c, f in zip(cand, flat)]
    expert, _ = _candidate_rows(i1, i2, lambda a, b: a * float(PEER_KEYS) + b, sub)
    unused_base = PEER_TOPK * PEER_TOPK
    ids = [jnp.where(f >= 0, f, unused_base + SUBLANES * g + sub).astype(F32) for g, f in enumerate(flat)]
    top = _select_top(cand, ids, expert, PEER_TOPK)
    ex = [jnp.exp(_pack([t[0] for t in top[o:o + SUBLANES]], sub) - top[0][0]) for o in (0, SUBLANES)]
    tot = _all_rows(ex, jnp.add)
    out = []
    for o, e in zip((0, SUBLANES), ex):
        code = _pack([t[2] for t in top[o:o + SUBLANES]], sub)
        e1 = jnp.floor(code * (1.0 / PEER_KEYS))
        out.append((e / tot, e1, code - e1 * float(PEER_KEYS)))
    return out


def _peer_head(h, qp_ref, k1_ref, k2_ref, e1t_scr, e2t_scr, gt_scr):
    s1 = _dot_nt(k1_ref[h], qp_ref[2 * h])
    s2 = _dot_nt(k2_ref[h], qp_ref[2 * h + 1])
    base = pl.multiple_of(h * PEER_TOPK, PEER_TOPK)
    for o, (gate, e1, e2) in zip((0, SUBLANES), _head_experts(s1, s2)):
        rows = pl.ds(pl.multiple_of(base + o, SUBLANES), SUBLANES)
        gt_scr[rows, :] = gate
        e1t_scr[rows, :] = e1
        e2t_scr[rows, :] = e2


def _topk_body(qp_ref, k1_ref, k2_ref, e1_ref, e2_ref, g_ref, e1t_scr, e2t_scr, gt_scr):
    def head_group(i, carry):
        for o in range(TOPK_HEADS_PER_ITER):
            _peer_head(TOPK_HEADS_PER_ITER * i + o, qp_ref, k1_ref, k2_ref, e1t_scr, e2t_scr, gt_scr)
        return carry

    lax.fori_loop(0, PEER_HEADS // TOPK_HEADS_PER_ITER, head_group, 0)
    e1_ref[...] = e1t_scr[...].T.astype(I32)
    e2_ref[...] = e2t_scr[...].T.astype(I32)
    g_ref[...] = gt_scr[...].T


def _topk(qp, k1, k2):
    nq, tokens, _ = qp.shape
    tm = TM_TOPK
    assert tm == PEER_SLOTS
    row = lambda i: (i, 0)
    const3 = lambda i: (0, 0, 0)
    return pl.pallas_call(
        _topk_body,
        out_shape=(
            jax.ShapeDtypeStruct((tokens, PEER_SLOTS), I32),
            jax.ShapeDtypeStruct((tokens, PEER_SLOTS), I32),
            jax.ShapeDtypeStruct((tokens, PEER_SLOTS), F32),
        ),
        grid=(tokens // tm,),
        in_specs=[
            pl.BlockSpec((nq, tm, PEER_HALF), lambda i: (0, i, 0)),
            pl.BlockSpec((PEER_HEADS, PEER_KEYS, PEER_HALF), const3),
            pl.BlockSpec((PEER_HEADS, PEER_KEYS, PEER_HALF), const3),
        ],
        out_specs=(
            pl.BlockSpec((tm, PEER_SLOTS), row),
            pl.BlockSpec((tm, PEER_SLOTS), row),
            pl.BlockSpec((tm, PEER_SLOTS), row),
        ),
        scratch_shapes=[
            pltpu.VMEM((PEER_SLOTS, tm), F32),
            pltpu.VMEM((PEER_SLOTS, tm), F32),
            pltpu.VMEM((PEER_SLOTS, tm), F32),
        ],
        compiler_params=pltpu.CompilerParams(
            dimension_semantics=("arbitrary",), vmem_limit_bytes=VMEM_LIMIT),
        name="topk",
    )(qp, k1, k2)


def _peer_u_body(h2_ref, ut_ref, e1_ref, e2_ref, g_ref, w_ref, acc_ref, sc_ref):
    j = pl.program_id(1)
    n_e1 = EC_U // PEER_KEYS

    @pl.when(j == 0)
    def _():
        acc_ref[...] = jnp.zeros_like(acc_ref)
        sc_ref[...] = jnp.zeros_like(sc_ref)

    prev = sc_ref.at[(j + 1) % 2]
    for r in range(0, h2_ref.shape[0], GATHER_ROWS):
        rows = slice(r, r + GATHER_ROWS)
        e1 = e1_ref[rows, :]
        e2 = e2_ref[rows, :]
        acc = acc_ref[rows, :]
        for c in range(n_e1):
            picked = jnp.take_along_axis(prev[rows, c * PEER_KEYS:(c + 1) * PEER_KEYS], e2, axis=1)
            acc = jnp.where(e1 == (j - 1) * n_e1 + c, picked, acc)
        acc_ref[rows, :] = acc
    sc_ref[j % 2] = _dot(h2_ref[...], ut_ref[...])

    @pl.when(j == pl.num_programs(1) - 1)
    def _():
        acc = acc_ref[...]
        w_ref[...] = g_ref[...] * (0.5 * acc * (1.0 + lax.erf(acc * np.float32(np.sqrt(0.5)))))


def _peer_u(h2, u_t, e1, e2, gate):
    tokens = h2.shape[0]
    tm = TM_U
    n_chunks = PEER_EXPERTS // EC_U
    row = lambda i, j: (i, 0)
    return pl.pallas_call(
        _peer_u_body,
        out_shape=jax.ShapeDtypeStruct((tokens, PEER_SLOTS), F32),
        grid=(tokens // tm, n_chunks + 1),
        in_specs=[
            pl.BlockSpec((tm, D_MODEL), row),
            pl.BlockSpec((D_MODEL, EC_U), lambda i, j: (0, jnp.minimum(j, n_chunks - 1))),
            pl.BlockSpec((tm, PEER_SLOTS), row),
            pl.BlockSpec((tm, PEER_SLOTS), row),
            pl.BlockSpec((tm, PEER_SLOTS), row),
        ],
        out_specs=pl.BlockSpec((tm, PEER_SLOTS), row),
        scratch_shapes=[pltpu.VMEM((tm, PEER_SLOTS), F32), pltpu.VMEM((2, tm, EC_U), F32)],
        compiler_params=pltpu.CompilerParams(
            dimension_semantics=("arbitrary", "arbitrary"), vmem_limit_bytes=VMEM_LIMIT),
        name="peer_u",
    )(h2, u_t, e1, e2, gate)


def _wd_pitch(tm):
    tiles = tm // 2 // SUBLANES
    return (tiles + 1 - tiles % 2) * SUBLANES


def _bf16_bits(x):
    return lax.bitcast_convert_type(x.astype(BF16).astype(F32), U32)


def _peer_v_body(e1_ref, e2_ref, w_ref, v_ref, x1_ref, y_ref, wd_scr):
    tm = e1_ref.shape[0]
    half = tm // 2
    pitch = _wd_pitch(tm)
    j = pl.program_id(1)

    @pl.when(j == 0)
    def _():
        eiota = lax.broadcasted_iota(I32, (PEER_KEYS, PEER_SLOTS), 0)

        def tile(t):
            e1 = e1_ref[pl.ds(t, 1), :]
            e2 = e2_ref[pl.ds(t, 1), :]
            wt = w_ref[pl.ds(t, 1), :]
            a = jnp.where(eiota == e1, wt, 0.0).astype(BF16)
            b = jnp.where(eiota == e2, 1.0, 0.0).astype(BF16)
            return _dot_nt(a, b)

        def token_pair(p, carry):
            packed = (_bf16_bits(tile(p)) >> 16) | _bf16_bits(tile(p + half))
            wd_scr[pl.ds(p, PEER_KEYS, stride=pitch), :] = packed
            return carry

        lax.fori_loop(0, half, token_pair, 0, unroll=32)
        y_ref[...] = x1_ref[...]

    n_e1 = EC_V // PEER_KEYS
    parts = []
    for c in range(n_e1):
        words = wd_scr[pl.ds(pl.multiple_of((j * n_e1 + c) * pitch, SUBLANES), half), :]
        lo = lax.bitcast_convert_type(words << 16, F32)
        hi = lax.bitcast_convert_type(words & jnp.uint32(0xFFFF0000), F32)
        parts.append(jnp.concatenate([lo, hi], axis=0))
    dense_w = jnp.concatenate(parts, axis=-1).astype(BF16)
    y_ref[...] += _dot(dense_w, v_ref[...])


def _peer_v(e1, e2, w, v_bf16, x1):
    tokens = x1.shape[0]
    tm = TM_V
    row = lambda i, j: (i, 0)
    return pl.pallas_call(
        _peer_v_body,
        out_shape=jax.ShapeDtypeStruct((tokens, D_MODEL), F32),
        grid=(tokens // tm, PEER_EXPERTS // EC_V),
        in_specs=[
            pl.BlockSpec((tm, PEER_SLOTS), row),
            pl.BlockSpec((tm, PEER_SLOTS), row),
            pl.BlockSpec((tm, PEER_SLOTS), row),
            pl.BlockSpec((EC_V, D_MODEL), lambda i, j: (j, 0)),
            pl.BlockSpec((tm, D_MODEL), row, pipeline_mode=pl.Buffered(1)),
        ],
        out_specs=pl.BlockSpec((tm, D_MODEL), row, pipeline_mode=pl.Buffered(1)),
        scratch_shapes=[pltpu.VMEM((PEER_KEYS * _wd_pitch(tm), PEER_KEYS), U32)],
        compiler_params=pltpu.CompilerParams(
            dimension_semantics=("arbitrary", "arbitrary"), vmem_limit_bytes=VMEM_LIMIT),
        name="peer_v",
    )(e1, e2, w, v_bf16, x1)


def _rope_tables(seq):
    inv = ROPE_THETA ** (-jnp.arange(0, ROPE_DIM, 2, dtype=F32) / ROPE_DIM)
    ang = jnp.arange(seq, dtype=F32)[:, None] * inv[None, :]
    cos = jnp.cos(ang)
    sin = jnp.sin(ang)
    lane = np.arange(LANES)
    in_head = lane % HEAD_DIM
    freq = in_head % ROPE_HALF
    cos_l = jnp.where(in_head < ROPE_DIM, cos[:, freq], 1.0)
    sin_l = jnp.where(in_head < ROPE_HALF, -sin[:, freq], jnp.where(in_head < ROPE_DIM, sin[:, freq], 0.0))
    return cos_l.astype(F32), sin_l.astype(F32)


def _layer(x, p):
    batch, seq, _ = x.shape
    tokens = batch * seq
    x2d = x.reshape(tokens, D_MODEL)
    cos_t, sin_t = _rope_tables(seq)
    outs = _inproj(x2d, seq, p["norm1"], p["w_in"], p["gqa"], p["gka"], p["gqb"], p["gkb"], cos_t, sin_t, p["bd"])
    qa, ka, va = outs[:3]
    qb, kb, vb = outs[3:3 + B_GROUPS], outs[3 + B_GROUPS:3 + 2 * B_GROUPS], outs[3 + 2 * B_GROUPS:3 + 3 * B_GROUPS]
    gates = outs[3 + 3 * B_GROUPS]

    oa, = _banded_attention(
        qa.reshape(batch, seq, A_Q_W), ka.reshape(batch, seq, A_KV_W), va.reshape(batch, seq, A_KV_W),
        window=A_WINDOW, nkv=A_KV_HEADS, grp=A_GROUP, sink=p["sink"], name="attn_a")

    o_groups, lse_groups = [], []
    for gi, (win, dil) in enumerate(B_PATTERNS):
        sub = seq // dil
        as_seqs = lambda t: t.reshape(batch * dil, sub, B_GROUP_W)
        o, lse = _banded_attention(
            as_seqs(qb[gi]), as_seqs(kb[gi]), as_seqs(vb[gi]), window=win // (2 * dil),
            nkv=B_HEADS_PER_GROUP, grp=1, want_lse=True, out_dtype=F32, name=f"attn_b{gi}")
        o_groups.append(o.reshape(batch, dil, sub, B_GROUP_W))
        lse_groups.append(lse.reshape(batch, dil, sub, B_GROUP_W))

    x1, h2, qp = _post(oa.reshape(tokens, A_Q_W), o_groups, lse_groups, gates, x2d, seq,
                       p["w_o_a"], p["w_o_b"], p["w_out"], p["norm2"], p["w_query"])
    e1, e2, gate = _topk(qp, p["k1"], p["k2"])
    w = _peer_u(h2, p["u_t"], e1, e2, gate)
    y = _peer_v(e1, e2, w, p["v"], x1)
    return y.reshape(batch, seq, D_MODEL)


def _tile_gain(g):
    return jnp.tile(g.astype(F32), LANES // HEAD_DIM).reshape(1, LANES)


def kernel(x_prompt, x_sample, norm1, w_in, q_norm_a, k_norm_a, sink_a, q_norm_b, k_norm_b, w_o_a, w_o_b, w_out,
           norm2, w_query, sub_keys_1, sub_keys_2, expert_u, expert_v):
    depth = norm1.shape[0]
    head_of_lane = np.arange(PREP_COLS) // HEAD_DIM
    bd = jnp.asarray((head_of_lane[:, None] == head_of_lane[None, :]) / HEAD_DIM, dtype=BF16)
    params = []
    for l in range(depth):
        params.append({
            "norm1": norm1[l].reshape(1, D_MODEL),
            "w_in": w_in[l].astype(BF16),
            "gqa": _tile_gain(q_norm_a[l]), "gka": _tile_gain(k_norm_a[l]),
            "gqb": _tile_gain(q_norm_b[l]), "gkb": _tile_gain(k_norm_b[l]),
            "sink": sink_a[l].astype(F32),
            "w_o_a": w_o_a[l].astype(BF16), "w_o_b": w_o_b[l].astype(BF16), "w_out": w_out[l].astype(BF16),
            "norm2": norm2[l].reshape(1, D_MODEL),
            "w_query": w_query[l].astype(BF16),
            "k1": sub_keys_1[l].astype(BF16), "k2": sub_keys_2[l].astype(BF16),
            "u_t": expert_u[l].astype(BF16).T,
            "v": expert_v[l].astype(BF16),
            "bd": bd,
        })
    ys = []
    for x in (x_prompt, x_sample):
        for p in params:
            x = _layer(x, p)
        ys.append(x)
    return tuple(ys)
```

```python
import functools

import numpy as np
import jax
import jax.numpy as jnp
from jax import lax
from jax.experimental import pallas as pl
from jax.experimental.pallas import tpu as pltpu

F32 = jnp.float32
BF16 = jnp.bfloat16
I32 = jnp.int32
U32 = jnp.uint32

D_MODEL = 1024
HEAD_DIM = 64
ROPE_DIM = HEAD_DIM // 4
ROPE_HALF = ROPE_DIM // 2
ROPE_THETA = 500000.0
EPS = 1e-6
A_Q_HEADS = 8
A_KV_HEADS = 2
A_GROUP = A_Q_HEADS // A_KV_HEADS
A_WINDOW = 128
B_PATTERNS = ((128, 1), (512, 4), (2048, 16))
B_HEADS_PER_GROUP = 4
B_GROUPS = len(B_PATTERNS)
A_Q_W = A_Q_HEADS * HEAD_DIM
A_KV_W = A_KV_HEADS * HEAD_DIM
B_GROUP_W = B_HEADS_PER_GROUP * HEAD_DIM
B_W = B_GROUPS * B_GROUP_W
IN_COLS = A_Q_W + 2 * A_KV_W + 3 * B_W + 2 * D_MODEL
PEER_HEADS = 8
PEER_KEYS = 128
PEER_EXPERTS = PEER_KEYS * PEER_KEYS
PEER_HALF = 128
PEER_TOPK = 16
PEER_SLOTS = PEER_HEADS * PEER_TOPK
PEER_Q_W = PEER_HEADS * 2 * PEER_HALF

LANES = 128
SUBLANES = 8
MASK_VALUE = -1e30
NEG = float(np.finfo(np.float32).min)
VMEM_LIMIT = 56 * 1024 * 1024

OFF_QA = 0
OFF_KA = OFF_QA + A_Q_W
OFF_VA = OFF_KA + A_KV_W
OFF_QB = OFF_VA + A_KV_W
OFF_KB = OFF_QB + B_W
OFF_VB = OFF_KB + B_W
OFF_GA = OFF_VB + B_W

TM_IN = 512
PREP_COLS = 256
TM_POST = 512
TM_TOPK = 128
TOPK_HEADS_PER_ITER = 8
TM_U = 2048
EC_U = 512
GATHER_ROWS = 8
TM_V = 1024
EC_V = 1024
ATTN_TQ = 256
ATTN_SUBQ = 256
ATTN_ROWS_PER_STEP = 512

def _dot(a, b):
    return jnp.dot(a, b, preferred_element_type=F32)


def _dot_nt(a, b):
    return lax.dot_general(a, b, (((1,), (1,)), ((), ())), preferred_element_type=F32)


def _rms(x, g):
    ms = jnp.mean(x * x, axis=-1, keepdims=True)
    return (x * lax.rsqrt(ms + EPS)) * g


def _head_prep(t, gain, cosv, sinv, bd, lane_lo):
    sq = t * t
    hi = sq.astype(BF16)
    lo = (sq - hi.astype(F32)).astype(BF16)
    ms = _dot(hi, bd) + _dot(lo, bd)
    tn = (t * lax.rsqrt(ms + EPS)) * gain
    out = []
    for c in range(t.shape[1] // LANES):
        tc = tn[:, c * LANES:(c + 1) * LANES]
        partner = jnp.where(lane_lo, pltpu.roll(tc, LANES - ROPE_HALF, 1), pltpu.roll(tc, ROPE_HALF, 1))
        out.append(tc * cosv + partner * sinv)
    return out


def _inproj_body(x_ref, n1_ref, w_ref, gqa_ref, gka_ref, gqb_ref, gkb_ref, cos_ref, sin_ref, bd_ref,
                 qa_ref, ka_ref, va_ref, *rest):
    qb_refs, kb_refs, vb_refs = rest[0:B_GROUPS], rest[B_GROUPS:2 * B_GROUPS], rest[2 * B_GROUPS:3 * B_GROUPS]
    gate_ref, scr = rest[3 * B_GROUPS], rest[3 * B_GROUPS + 1]
    tm = x_ref.shape[0]
    h = _rms(x_ref[...], n1_ref[...]).astype(BF16)
    cosv = cos_ref[...]
    sinv = sin_ref[...]
    bd = bd_ref[...]
    lane = lax.broadcasted_iota(I32, (1, LANES), 1)
    lane_lo = (lane % HEAD_DIM) < ROPE_HALF

    def prepped(off, width, gain_ref, scale, emit):
        step = min(width, PREP_COLS)
        gain = jnp.concatenate([gain_ref[...]] * (step // LANES), axis=-1)
        for c in range(width // step):
            t = _dot(h, w_ref[:, off + c * step: off + (c + 1) * step])
            chunks = _head_prep(t, gain, cosv, sinv, bd[:step, :step], lane_lo)
            emit(c, [tk * scale if scale != 1.0 else tk for tk in chunks])

    def flat(out_ref):
        def emit(c, chunks):
            for k, tk in enumerate(chunks):
                lo = c * PREP_COLS + k * LANES
                out_ref[:, lo:lo + LANES] = tk.astype(out_ref.dtype)
        return emit

    def by_residue(out_refs):
        def emit(g, chunks):
            dil = B_PATTERNS[g][1]
            for k, tk in enumerate(chunks):
                cols = slice(k * LANES, (k + 1) * LANES)
                if dil == 1:
                    out_refs[g][0, :, cols] = tk.astype(out_refs[g].dtype)
                    continue
                scr[k] = tk
                for r in range(dil):
                    rows = scr[k, pl.ds(r, tm // dil, stride=dil), :]
                    out_refs[g][r, :, cols] = rows.astype(out_refs[g].dtype)
        return emit

    assert PREP_COLS == B_GROUP_W
    qscale = HEAD_DIM ** -0.5
    prepped(OFF_QA, A_Q_W, gqa_ref, qscale, flat(qa_ref))
    prepped(OFF_KA, A_KV_W, gka_ref, 1.0, flat(ka_ref))
    va_ref[...] = _dot(h, w_ref[:, OFF_VA:OFF_VA + A_KV_W]).astype(va_ref.dtype)
    prepped(OFF_QB, B_W, gqb_ref, qscale, by_residue(qb_refs))
    prepped(OFF_KB, B_W, gkb_ref, 1.0, by_residue(kb_refs))
    for g in range(B_GROUPS):
        t = _dot(h, w_ref[:, OFF_VB + g * B_GROUP_W: OFF_VB + (g + 1) * B_GROUP_W])
        by_residue(vb_refs)(g, [t[:, k * LANES:(k + 1) * LANES] for k in range(B_GROUP_W // LANES)])
    for c in range(2 * D_MODEL // 512):
        g = _dot(h, w_ref[:, OFF_GA + c * 512: OFF_GA + (c + 1) * 512])
        gate_ref[:, c * 512:(c + 1) * 512] = jax.nn.sigmoid(g)


def _inproj(x2d, seq, n1, w_in, gqa, gka, gqb, gkb, cos_t, sin_t, bd):
    tokens = x2d.shape[0]
    tm = TM_IN
    blocks_per_seq = seq // tm
    row = lambda i: (i, 0)
    const = lambda i: (0, 0)
    pos = lambda i: (i % blocks_per_seq, 0)
    batch = tokens // seq
    group_shapes = [jax.ShapeDtypeStruct((batch, dil, seq // dil, B_GROUP_W), BF16) for _, dil in B_PATTERNS]
    group_specs = [pl.BlockSpec((None, dil, tm // dil, B_GROUP_W),
                                lambda i: (i // blocks_per_seq, 0, i % blocks_per_seq, 0)) for _, dil in B_PATTERNS]
    out_shape = (
        jax.ShapeDtypeStruct((tokens, A_Q_W), BF16),
        jax.ShapeDtypeStruct((tokens, A_KV_W), BF16),
        jax.ShapeDtypeStruct((tokens, A_KV_W), BF16),
        *group_shapes, *group_shapes, *group_shapes,
        jax.ShapeDtypeStruct((tokens, 2 * D_MODEL), F32),
    )
    return pl.pallas_call(
        _inproj_body,
        out_shape=out_shape,
        scratch_shapes=[pltpu.VMEM((B_GROUP_W // LANES, tm, LANES), F32)],
        grid=(tokens // tm,),
        in_specs=[
            pl.BlockSpec((tm, D_MODEL), row),
            pl.BlockSpec((1, D_MODEL), const),
            pl.BlockSpec((D_MODEL, IN_COLS), const),
            pl.BlockSpec((1, LANES), const),
            pl.BlockSpec((1, LANES), const),
            pl.BlockSpec((1, LANES), const),
            pl.BlockSpec((1, LANES), const),
            pl.BlockSpec((tm, LANES), pos),
            pl.BlockSpec((tm, LANES), pos),
            pl.BlockSpec((PREP_COLS, PREP_COLS), const),
        ],
        out_specs=(
            pl.BlockSpec((tm, A_Q_W), row),
            pl.BlockSpec((tm, A_KV_W), row),
            pl.BlockSpec((tm, A_KV_W), row),
            *group_specs, *group_specs, *group_specs,
            pl.BlockSpec((tm, 2 * D_MODEL), row),
        ),
        compiler_params=pltpu.CompilerParams(
            dimension_semantics=("arbitrary",), vmem_limit_bytes=VMEM_LIMIT),
        name="inproj",
    )(x2d, n1, w_in, gqa, gka, gqb, gkb, cos_t, sin_t, bd)


def _attn_body(*refs, seqs_per_step, has_sink, **params):
    refs = list(refs)
    sink_ref = refs.pop(0) if has_sink else None
    for s in range(seqs_per_step):
        _attn_one(sink_ref, *[r.at[s] for r in refs], **params)


def _attn_one(sink_ref, q_ref, k_ref, v_ref, o_ref, lse_ref=None, *, window, tq, kw, length, nkv, grp, want_lse):
    has_sink = sink_ref is not None

    q0 = pl.program_id(1) * tq
    if kw == length:
        start = 0
        kwin = k_ref[...]
        vwin = v_ref[...]
    else:
        start = pl.multiple_of(jnp.clip(q0 - window, 0, length - kw), window)
        kwin = k_ref[pl.ds(start, kw), :]
        vwin = v_ref[pl.ds(start, kw), :]
    sq = min(tq, ATTN_SUBQ)
    qi = lax.broadcasted_iota(I32, (sq, kw), 0)
    kj = lax.broadcasted_iota(I32, (sq, kw), 1)
    rel = kj - qi + (start - q0)
    valid = [jnp.abs(rel - r0) <= window for r0 in range(0, tq, sq)]

    o_parts = [None] * (nkv * grp)
    lse_parts = [None] * (nkv * grp)
    for j in range(nkv):
        kh = kwin[:, j * HEAD_DIM:(j + 1) * HEAD_DIM]
        vh = vwin[:, j * HEAD_DIM:(j + 1) * HEAD_DIM]
        for g in range(grp):
            h = j * grp + g
            o_rows, lse_rows = [], []
            for i, r0 in enumerate(range(0, tq, sq)):
                qh = q_ref[r0:r0 + sq, h * HEAD_DIM:(h + 1) * HEAD_DIM]
                s = jnp.where(valid[i], _dot_nt(qh, kh), MASK_VALUE)
                m = jnp.max(s, axis=-1, keepdims=True)
                if has_sink:
                    m = jnp.maximum(m, sink_ref[h])
                p = jnp.exp(s - m)
                denom = jnp.sum(p, axis=-1, keepdims=True)
                if has_sink:
                    denom = denom + jnp.exp(sink_ref[h] - m)
                o_rows.append(_dot(p.astype(BF16), vh) / denom)
                if want_lse:
                    lse_rows.append(jnp.broadcast_to(m + jnp.log(denom), (sq, HEAD_DIM)))
            o_parts[h] = o_rows[0] if len(o_rows) == 1 else jnp.concatenate(o_rows, axis=0)
            if want_lse:
                lse_parts[h] = lse_rows[0] if len(lse_rows) == 1 else jnp.concatenate(lse_rows, axis=0)
    o_ref[...] = jnp.concatenate(o_parts, axis=-1).astype(o_ref.dtype)
    if want_lse:
        lse_ref[...] = jnp.concatenate(lse_parts, axis=-1)


def _banded_attention(q, k, v, *, window, nkv, grp, sink=None, want_lse=False, out_dtype=BF16, name):
    nseq, length, wq = q.shape
    wk = nkv * HEAD_DIM
    tq = min(ATTN_TQ, length)
    kw = min(length, tq + 2 * window)
    sps = max(1, ATTN_ROWS_PER_STEP // length)
    body = functools.partial(_attn_body, seqs_per_step=sps, window=window, tq=tq, kw=kw, length=length, nkv=nkv,
                             grp=grp, has_sink=sink is not None, want_lse=want_lse)
    in_specs = [
        pl.BlockSpec((sps, tq, wq), lambda b, t: (b, t, 0)),
        pl.BlockSpec((sps, length, wk), lambda b, t: (b, 0, 0)),
        pl.BlockSpec((sps, length, wk), lambda b, t: (b, 0, 0)),
    ]
    args = [q, k, v]
    if sink is not None:
        in_specs.insert(0, pl.BlockSpec(memory_space=pltpu.SMEM))
        args.insert(0, sink)
    o_spec = pl.BlockSpec((sps, tq, wq), lambda b, t: (b, t, 0))
    out_shape = [jax.ShapeDtypeStruct((nseq, length, wq), out_dtype)]
    out_specs = [o_spec]
    if want_lse:
        out_shape.append(jax.ShapeDtypeStruct((nseq, length, wq), F32))
        out_specs.append(o_spec)
    return pl.pallas_call(
        body,
        out_shape=tuple(out_shape),
        grid=(nseq // sps, length // tq),
        in_specs=in_specs,
        out_specs=tuple(out_specs),
        compiler_params=pltpu.CompilerParams(
            dimension_semantics=("arbitrary", "arbitrary"), vmem_limit_bytes=VMEM_LIMIT),
        name=name,
    )(*args)


def _post_body(oa_ref, o1_ref, o2_ref, o3_ref, l1_ref, l2_ref, l3_ref, ga_ref, gb_ref, x_ref,
               woa_ref, wob_ref, wout_ref, n2_ref, wq_ref,
               x1_ref, h2_ref, qp_ref, scr):
    tm = x_ref.shape[0]

    def tokens_major(ref, g):
        dil = B_PATTERNS[g][1]
        if dil == 1:
            return ref[0]
        halves = []
        for k in range(B_GROUP_W // LANES):
            for r in range(dil):
                scr[k, pl.ds(r, tm // dil, stride=dil), :] = ref[r, :, k * LANES:(k + 1) * LANES]
            halves.append(scr[k])
        return jnp.concatenate(halves, axis=-1)

    l1, l2, l3 = [tokens_major(ref, g) for g, ref in enumerate((l1_ref, l2_ref, l3_ref))]
    mx = jnp.maximum(jnp.maximum(l1, l2), l3)
    e1, e2, e3 = jnp.exp(l1 - mx), jnp.exp(l2 - mx), jnp.exp(l3 - mx)
    tot = e1 + e2 + e3
    o1, o2, o3 = [tokens_major(ref, g) for g, ref in enumerate((o1_ref, o2_ref, o3_ref))]
    ob = (e1 / tot) * o1 + (e2 / tot) * o2 + (e3 / tot) * o3
    ya = _dot(oa_ref[...], woa_ref[...])
    yb = _dot(ob.astype(BF16), wob_ref[...])
    merged = ga_ref[...] * ya + gb_ref[...] * yb
    x1 = x_ref[...] + _dot(merged.astype(BF16), wout_ref[...])
    x1_ref[...] = x1
    h2 = _rms(x1, n2_ref[...]).astype(BF16)
    h2_ref[...] = h2
    for c in range(PEER_Q_W // PEER_HALF):
        qp_ref[c] = _dot(h2, wq_ref[:, c * PEER_HALF:(c + 1) * PEER_HALF]).astype(qp_ref.dtype)


def _post(oa, o_groups, lse_groups, gates, x2d, seq, w_o_a, w_o_b, w_out, n2, w_query):
    tokens = x2d.shape[0]
    tm = TM_POST
    blocks_per_seq = seq // tm
    row = lambda i: (i, 0)
    const = lambda i: (0, 0)
    nq = PEER_Q_W // PEER_HALF
    group_specs = [pl.BlockSpec((None, dil, tm // dil, B_GROUP_W),
                                lambda i: (i // blocks_per_seq, 0, i % blocks_per_seq, 0)) for _, dil in B_PATTERNS]
    return pl.pallas_call(
        _post_body,
        out_shape=(
            jax.ShapeDtypeStruct((tokens, D_MODEL), F32),
            jax.ShapeDtypeStruct((tokens, D_MODEL), BF16),
            jax.ShapeDtypeStruct((nq, tokens, PEER_HALF), BF16),
        ),
        grid=(tokens // tm,),
        scratch_shapes=[pltpu.VMEM((B_GROUP_W // LANES, tm, LANES), F32)],
        in_specs=[pl.BlockSpec((tm, A_Q_W), row)] + group_specs + group_specs
        + [
            pl.BlockSpec((tm, D_MODEL), lambda i: (i, 0)),
            pl.BlockSpec((tm, D_MODEL), lambda i: (i, 1)),
            pl.BlockSpec((tm, D_MODEL), row),
            pl.BlockSpec((A_Q_W, D_MODEL), const),
            pl.BlockSpec((B_GROUP_W, D_MODEL), const),
            pl.BlockSpec((D_MODEL, D_MODEL), const),
            pl.BlockSpec((1, D_MODEL), const),
            pl.BlockSpec((D_MODEL, PEER_Q_W), const),
        ],
        out_specs=(
            pl.BlockSpec((tm, D_MODEL), row),
            pl.BlockSpec((tm, D_MODEL), row),
            pl.BlockSpec((nq, tm, PEER_HALF), lambda i: (0, i, 0)),
        ),
        compiler_params=pltpu.CompilerParams(
            dimension_semantics=("arbitrary",), vmem_limit_bytes=VMEM_LIMIT),
        name="post",
    )(oa, *o_groups, *lse_groups, gates, gates, x2d, w_o_a, w_o_b, w_out, n2, w_query)


def _tree(xs, op):
    xs = list(xs)
    while len(xs) > 1:
        nxt = [op(xs[i], xs[i + 1]) for i in range(0, len(xs) - 1, 2)]
        if len(xs) % 2:
            nxt.append(xs[-1])
        xs = nxt
    return xs[0]


def _all_rows(blocks, op):
    r = _tree(blocks, op)
    for shift in (4, 2, 1):
        r = op(r, pltpu.roll(r, shift, 0))
    return r


def _pack(rows, sub):
    blk = rows[0]
    for i in range(1, SUBLANES):
        blk = jnp.where(sub == i, rows[i], blk)
    return blk


def _select_top(blocks, ids, payload, k):
    past = float(np.finfo(np.float32).max)
    out = []
    for _ in range(k):
        m = _all_rows(blocks, jnp.maximum)
        pos = _all_rows([jnp.where(b == m, i, past) for b, i in zip(blocks, ids)], jnp.minimum)
        sel = [i == pos for i in ids]
        picked = None
        if payload is not None:
            picked = _all_rows([jnp.where(s, p, -1.0) for s, p in zip(sel, payload)], jnp.maximum)
        blocks = [jnp.where(s, NEG, b) for s, b in zip(sel, blocks)]
        out.append((m, pos, picked))
    return out


def _candidate_rows(first, second, combine, sub):
    half = SUBLANES
    first_hi = pltpu.roll(_pack(first[half:], sub), 4, 0)
    lo = _pack(second[:half], sub)
    hi = _pack(second[half:], sub)
    w = jnp.where
    a_blocks = [first[0], first[0], first[1],
                w(sub < 5, first[2], first[5]),
                w(sub < 4, first[3], first[4]),
                w(sub < 2, first[6], w(sub < 4, first[7], first_hi)),
                first_hi]
    b_blocks = [lo, hi, lo,
                w(sub < 5, lo, pltpu.roll(lo, 5, 0)),
                w(sub < 4, lo, pltpu.roll(lo, 4, 0)),
                w(sub < 2, lo, w(sub < 4, pltpu.roll(lo, 2, 0), second[0])),
                second[0]]
    t = PEER_TOPK
    flat = [sub, half + sub, t + sub,
            w(sub < 5, 2 * t + sub, 5 * t + sub - 5),
            w(sub < 4, 3 * t + sub, 4 * t + sub - 4),
            w(sub < 2, 6 * t + sub, w(sub < 4, 7 * t + sub - 2, (half + sub - 4) * t)),
            (half + 4 + sub) * t]
    used = [None, None, None, sub < 7, sub < 7, None, sub < 4]
    flat = [f if u is None else w(u, f, -1) for f, u in zip(flat, used)]
    return [combine(a, b) for a, b in zip(a_blocks, b_blocks)], flat


def _head_experts(s1, s2):
    tm = s1.shape[1]
    sub = lax.broadcasted_iota(I32, (SUBLANES, tm), 0)
    subf = sub.astype(F32)
    key_ids = [subf + float(SUBLANES * g) for g in range(PEER_KEYS // SUBLANES)]

    def half(s):
        top = _select_top([s[SUBLANES * g:SUBLANES * (g + 1)] for g in range(PEER_KEYS // SUBLANES)], key_ids,
                          None, PEER_TOPK)
        return [t[0] for t in top], [t[1] for t in top]

    v1, i1 = half(s1)
    v2, i2 = half(s2)
    cand, flat = _candidate_rows(v1, v2, lambda a, b: a + b, sub)
    cand = [jnp.where(f >= 0, c, NEG) for c, f in zip(cand, flat)]
    expert, _ = _candidate_rows(i1, i2, lambda a, b: a * float(PEER_KEYS) + b, sub)
    unused_base = PEER_TOPK * PEER_TOPK
    ids = [jnp.where(f >= 0, f, unused_base + SUBLANES * g + sub).astype(F32) for g, f in enumerate(flat)]
    top = _select_top(cand, ids, expert, PEER_TOPK)
    ex = [jnp.exp(_pack([t[0] for t in top[o:o + SUBLANES]], sub) - top[0][0]) for o in (0, SUBLANES)]
    tot = _all_rows(ex, jnp.add)
    out = []
    for o, e in zip((0, SUBLANES), ex):
        code = _pack([t[2] for t in top[o:o + SUBLANES]], sub)
        e1 = jnp.floor(code * (1.0 / PEER_KEYS))
        out.append((e / tot, e1, code - e1 * float(PEER_KEYS)))
    return out


def _peer_head(h, qp_ref, k1_ref, k2_ref, e1t_scr, e2t_scr, gt_scr):
    s1 = _dot_nt(k1_ref[h], qp_ref[2 * h])
    s2 = _dot_nt(k2_ref[h], qp_ref[2 * h + 1])
    base = pl.multiple_of(h * PEER_TOPK, PEER_TOPK)
    for o, (gate, e1, e2) in zip((0, SUBLANES), _head_experts(s1, s2)):
        rows = pl.ds(pl.multiple_of(base + o, SUBLANES), SUBLANES)
        gt_scr[rows, :] = gate
        e1t_scr[rows, :] = e1
        e2t_scr[rows, :] = e2


def _topk_body(qp_ref, k1_ref, k2_ref, e1_ref, e2_ref, g_ref, e1t_scr, e2t_scr, gt_scr):
    def head_group(i, carry):
        for o in range(TOPK_HEADS_PER_ITER):
            _peer_head(TOPK_HEADS_PER_ITER * i + o, qp_ref, k1_ref, k2_ref, e1t_scr, e2t_scr, gt_scr)
        return carry

    lax.fori_loop(0, PEER_HEADS // TOPK_HEADS_PER_ITER, head_group, 0)
    e1_ref[...] = e1t_scr[...].T.astype(I32)
    e2_ref[...] = e2t_scr[...].T.astype(I32)
    g_ref[...] = gt_scr[...].T


def _topk(qp, k1, k2):
    nq, tokens, _ = qp.shape
    tm = TM_TOPK
    assert tm == PEER_SLOTS
    row = lambda i: (i, 0)
    const3 = lambda i: (0, 0, 0)
    return pl.pallas_call(
        _topk_body,
        out_shape=(
            jax.ShapeDtypeStruct((tokens, PEER_SLOTS), I32),
            jax.ShapeDtypeStruct((tokens, PEER_SLOTS), I32),
            jax.ShapeDtypeStruct((tokens, PEER_SLOTS), F32),
        ),
        grid=(tokens // tm,),
        in_specs=[
            pl.BlockSpec((nq, tm, PEER_HALF), lambda i: (0, i, 0)),
            pl.BlockSpec((PEER_HEADS, PEER_KEYS, PEER_HALF), const3),
            pl.BlockSpec((PEER_HEADS, PEER_KEYS, PEER_HALF), const3),
        ],
        out_specs=(
            pl.BlockSpec((tm, PEER_SLOTS), row),
            pl.BlockSpec((tm, PEER_SLOTS), row),
            pl.BlockSpec((tm, PEER_SLOTS), row),
        ),
        scratch_shapes=[
            pltpu.VMEM((PEER_SLOTS, tm), F32),
            pltpu.VMEM((PEER_SLOTS, tm), F32),
            pltpu.VMEM((PEER_SLOTS, tm), F32),
        ],
        compiler_params=pltpu.CompilerParams(
            dimension_semantics=("arbitrary",), vmem_limit_bytes=VMEM_LIMIT),
        name="topk",
    )(qp, k1, k2)


def _peer_u_body(h2_ref, ut_ref, e1_ref, e2_ref, g_ref, w_ref, acc_ref, sc_ref):
    j = pl.program_id(1)
    n_e1 = EC_U // PEER_KEYS

    @pl.when(j == 0)
    def _():
        acc_ref[...] = jnp.zeros_like(acc_ref)
        sc_ref[...] = jnp.zeros_like(sc_ref)

    prev = sc_ref.at[(j + 1) % 2]
    for r in range(0, h2_ref.shape[0], GATHER_ROWS):
        rows = slice(r, r + GATHER_ROWS)
        e1 = e1_ref[rows, :]
        e2 = e2_ref[rows, :]
        acc = acc_ref[rows, :]
        for c in range(n_e1):
            picked = jnp.take_along_axis(prev[rows, c * PEER_KEYS:(c + 1) * PEER_KEYS], e2, axis=1)
            acc = jnp.where(e1 == (j - 1) * n_e1 + c, picked, acc)
        acc_ref[rows, :] = acc
    sc_ref[j % 2] = _dot(h2_ref[...], ut_ref[...])

    @pl.when(j == pl.num_programs(1) - 1)
    def _():
        acc = acc_ref[...]
        w_ref[...] = g_ref[...] * (0.5 * acc * (1.0 + lax.erf(acc * np.float32(np.sqrt(0.5)))))


def _peer_u(h2, u_t, e1, e2, gate):
    tokens = h2.shape[0]
    tm = TM_U
    n_chunks = PEER_EXPERTS // EC_U
    row = lambda i, j: (i, 0)
    return pl.pallas_call(
        _peer_u_body,
        out_shape=jax.ShapeDtypeStruct((tokens, PEER_SLOTS), F32),
        grid=(tokens // tm, n_chunks + 1),
        in_specs=[
            pl.BlockSpec((tm, D_MODEL), row),
            pl.BlockSpec((D_MODEL, EC_U), lambda i, j: (0, jnp.minimum(j, n_chunks - 1))),
            pl.BlockSpec((tm, PEER_SLOTS), row),
            pl.BlockSpec((tm, PEER_SLOTS), row),
            pl.BlockSpec((tm, PEER_SLOTS), row),
        ],
        out_specs=pl.BlockSpec((tm, PEER_SLOTS), row),
        scratch_shapes=[pltpu.VMEM((tm, PEER_SLOTS), F32), pltpu.VMEM((2, tm, EC_U), F32)],
        compiler_params=pltpu.CompilerParams(
            dimension_semantics=("arbitrary", "arbitrary"), vmem_limit_bytes=VMEM_LIMIT),
        name="peer_u",
    )(h2, u_t, e1, e2, gate)


def _wd_pitch(tm):
    tiles = tm // 2 // SUBLANES
    return (tiles + 1 - tiles % 2) * SUBLANES


def _bf16_bits(x):
    return lax.bitcast_convert_type(x.astype(BF16).astype(F32), U32)


def _peer_v_body(e1_ref, e2_ref, w_ref, v_ref, x1_ref, y_ref, wd_scr):
    tm = e1_ref.shape[0]
    half = tm // 2
    pitch = _wd_pitch(tm)
    j = pl.program_id(1)

    @pl.when(j == 0)
    def _():
        eiota = lax.broadcasted_iota(I32, (PEER_KEYS, PEER_SLOTS), 0)

        def tile(t):
            e1 = e1_ref[pl.ds(t, 1), :]
            e2 = e2_ref[pl.ds(t, 1), :]
            wt = w_ref[pl.ds(t, 1), :]
            a = jnp.where(eiota == e1, wt, 0.0).astype(BF16)
            b = jnp.where(eiota == e2, 1.0, 0.0).astype(BF16)
            return _dot_nt(a, b)

        def token_pair(p, carry):
            packed = (_bf16_bits(tile(p)) >> 16) | _bf16_bits(tile(p + half))
            wd_scr[pl.ds(p, PEER_KEYS, stride=pitch), :] = packed
            return carry

        lax.fori_loop(0, half, token_pair, 0, unroll=32)
        y_ref[...] = x1_ref[...]

    n_e1 = EC_V // PEER_KEYS
    parts = []
    for c in range(n_e1):
        words = wd_scr[pl.ds(pl.multiple_of((j * n_e1 + c) * pitch, SUBLANES), half), :]
        lo = lax.bitcast_convert_type(words << 16, F32)
        hi = lax.bitcast_convert_type(words & jnp.uint32(0xFFFF0000), F32)
        parts.append(jnp.concatenate([lo, hi], axis=0))
    dense_w = jnp.concatenate(parts, axis=-1).astype(BF16)
    y_ref[...] += _dot(dense_w, v_ref[...])


def _peer_v(e1, e2, w, v_bf16, x1):
    tokens = x1.shape[0]
    tm = TM_V
    row = lambda i, j: (i, 0)
    return pl.pallas_call(
        _peer_v_body,
        out_shape=jax.ShapeDtypeStruct((tokens, D_MODEL), F32),
        grid=(tokens // tm, PEER_EXPERTS // EC_V),
        in_specs=[
            pl.BlockSpec((tm, PEER_SLOTS), row),
            pl.BlockSpec((tm, PEER_SLOTS), row),
            pl.BlockSpec((tm, PEER_SLOTS), row),
            pl.BlockSpec((EC_V, D_MODEL), lambda i, j: (j, 0)),
            pl.BlockSpec((tm, D_MODEL), row, pipeline_mode=pl.Buffered(1)),
        ],
        out_specs=pl.BlockSpec((tm, D_MODEL), row, pipeline_mode=pl.Buffered(1)),
        scratch_shapes=[pltpu.VMEM((PEER_KEYS * _wd_pitch(tm), PEER_KEYS), U32)],
        compiler_params=pltpu.CompilerParams(
            dimension_semantics=("arbitrary", "arbitrary"), vmem_limit_bytes=VMEM_LIMIT),
        name="peer_v",
    )(e1, e2, w, v_bf16, x1)


def _rope_tables(seq):
    inv = ROPE_THETA ** (-jnp.arange(0, ROPE_DIM, 2, dtype=F32) / ROPE_DIM)
    ang = jnp.arange(seq, dtype=F32)[:, None] * inv[None, :]
    cos = jnp.cos(ang)
    sin = jnp.sin(ang)
    lane = np.arange(LANES)
    in_head = lane % HEAD_DIM
    freq = in_head % ROPE_HALF
    cos_l = jnp.where(in_head < ROPE_DIM, cos[:, freq], 1.0)
    sin_l = jnp.where(in_head < ROPE_HALF, -sin[:, freq], jnp.where(in_head < ROPE_DIM, sin[:, freq], 0.0))
    return cos_l.astype(F32), sin_l.astype(F32)


def _layer(x, p):
    batch, seq, _ = x.shape
    tokens = batch * seq
    x2d = x.reshape(tokens, D_MODEL)
    cos_t, sin_t = _rope_tables(seq)
    outs = _inproj(x2d, seq, p["norm1"], p["w_in"], p["gqa"], p["gka"], p["gqb"], p["gkb"], cos_t, sin_t, p["bd"])
    qa, ka, va = outs[:3]
    qb, kb, vb = outs[3:3 + B_GROUPS], outs[3 + B_GROUPS:3 + 2 * B_GROUPS], outs[3 + 2 * B_GROUPS:3 + 3 * B_GROUPS]
    gates = outs[3 + 3 * B_GROUPS]

    oa, = _banded_attention(
        qa.reshape(batch, seq, A_Q_W), ka.reshape(batch, seq, A_KV_W), va.reshape(batch, seq, A_KV_W),
        window=A_WINDOW, nkv=A_KV_HEADS, grp=A_GROUP, sink=p["sink"], name="attn_a")

    o_groups, lse_groups = [], []
    for gi, (win, dil) in enumerate(B_PATTERNS):
        sub = seq // dil
        as_seqs = lambda t: t.reshape(batch * dil, sub, B_GROUP_W)
        o, lse = _banded_attention(
            as_seqs(qb[gi]), as_seqs(kb[gi]), as_seqs(vb[gi]), window=win // (2 * dil),
            nkv=B_HEADS_PER_GROUP, grp=1, want_lse=True, out_dtype=F32, name=f"attn_b{gi}")
        o_groups.append(o.reshape(batch, dil, sub, B_GROUP_W))
        lse_groups.append(lse.reshape(batch, dil, sub, B_GROUP_W))

    x1, h2, qp = _post(oa.reshape(tokens, A_Q_W), o_groups, lse_groups, gates, x2d, seq,
                       p["w_o_a"], p["w_o_b"], p["w_out"], p["norm2"], p["w_query"])
    e1, e2, gate = _topk(qp, p["k1"], p["k2"])
    w = _peer_u(h2, p["u_t"], e1, e2, gate)
    y = _peer_v(e1, e2, w, p["v"], x1)
    return y.reshape(batch, seq, D_MODEL)


def _tile_gain(g):
    return jnp.tile(g.astype(F32), LANES // HEAD_DIM).reshape(1, LANES)


def kernel(x_prompt, x_sample, norm1, w_in, q_norm_a, k_norm_a, sink_a, q_norm_b, k_norm_b, w_o_a, w_o_b, w_out,
           norm2, w_query, sub_keys_1, sub_keys_2, expert_u, expert_v):
    depth = norm1.shape[0]
    head_of_lane = np.arange(PREP_COLS) // HEAD_DIM
    bd = jnp.asarray((head_of_lane[:, None] == head_of_lane[None, :]) / HEAD_DIM, dtype=BF16)
    params = []
    for l in range(depth):
        params.append({
            "norm1": norm1[l].reshape(1, D_MODEL),
            "w_in": w_in[l].astype(BF16),
            "gqa": _tile_gain(q_norm_a[l]), "gka": _tile_gain(k_norm_a[l]),
            "gqb": _tile_gain(q_norm_b[l]), "gkb": _tile_gain(k_norm_b[l]),
            "sink": sink_a[l].astype(F32),
            "w_o_a": w_o_a[l].astype(BF16), "w_o_b": w_o_b[l].astype(BF16), "w_out": w_out[l].astype(BF16),
            "norm2": norm2[l].reshape(1, D_MODEL),
            "w_query": w_query[l].astype(BF16),
            "k1": sub_keys_1[l].astype(BF16), "k2": sub_keys_2[l].astype(BF16),
            "u_t": expert_u[l].astype(BF16).T,
            "v": expert_v[l].astype(BF16),
            "bd": bd,
        })
    ys = []
    for x in (x_prompt, x_sample):
        for p in params:
            x = _layer(x, p)
        ys.append(x)
    return tuple(ys)
```
